```python
import functools
import jax
import jax.numpy as jnp
from jax import lax
import numpy as np

D_MODEL = 4096
BATCH = 32
SEQ = 256
DEPTH = 4
DEC_BATCH = 8
DEC_SEQ = 4096
PAST_LEN = 256

GRID_W = 64
HEAD_DIM = 128
N_MIX_HEADS = D_MODEL // HEAD_DIM
HGRN_HEADS = 3 * N_MIX_HEADS // 8
MLSTM_HEADS = 3 * N_MIX_HEADS // 8
CMLP_GROUPS = N_MIX_HEADS - HGRN_HEADS - MLSTM_HEADS
HGRN_WIDTH = HGRN_HEADS * HEAD_DIM
MLSTM_WIDTH = MLSTM_HEADS * HEAD_DIM
CMLP_WIDTH = CMLP_GROUPS * HEAD_DIM
MIX_WIDTH = HGRN_WIDTH + MLSTM_WIDTH + CMLP_WIDTH
HGRN_CHUNK = 32
MLSTM_CHUNK = 64
CMLP_CHUNK = 2 * GRID_W
D_FF = ((8 * D_MODEL // 3 + 255) // 256) * 256
MLSTM_F_BIAS = 3.0
EPS = 1e-6
PROJ_SIZES = (HGRN_WIDTH,) * 5 + (MLSTM_WIDTH,) * 4 + (4 * MLSTM_HEADS,) + (CMLP_WIDTH,) * 2
PROJ_WIDTH = sum(PROJ_SIZES)
PROJ_SPLITS = tuple(sum(PROJ_SIZES[:i + 1]) for i in range(len(PROJ_SIZES) - 1))

kernel_name = "hybrid_hgrn2_mlstm_cmlp_diffusion_step"


def rms_norm(x, g):
    xf = x.astype(jnp.float32)
    y = xf * lax.rsqrt(jnp.mean(xf * xf, axis=-1, keepdims=True) + EPS)
    return (y * g.astype(jnp.float32)).astype(x.dtype)


def _chunks(t, size):
    return jnp.moveaxis(t.reshape(t.shape[0], t.shape[1] // size, size, *t.shape[2:]), 1, 0)


def _unchunks(t):
    t = jnp.moveaxis(t, 0, 1)
    return t.reshape(t.shape[0], t.shape[1] * t.shape[2], *t.shape[3:])


def hgrn2_chunked(q, k, v, log_f, s0):
    f32 = jnp.float32
    size = HGRN_CHUNK
    mask = jnp.tril(jnp.ones((size, size), f32))

    def step(s, inp):
        qc, kc, vc, gc = inp
        b = jnp.cumsum(gc, axis=1)
        b_end = b[:, -1]
        q_dec = qc * jnp.exp(b)
        k_inv = kc * jnp.exp(-b)
        att = jnp.einsum('bthk,bshk->bhts', q_dec, k_inv) * mask
        o = (jnp.einsum('bthk,bhkv->bthv', q_dec, s)
             + jnp.einsum('bhts,bshv->bthv', att, vc))
        k_end = kc * jnp.exp(b_end[:, None] - b)
        s_new = jnp.exp(b_end)[..., None] * s + jnp.einsum('bshk,bshv->bhkv', k_end, vc)
        return s_new, o

    xs = tuple(_chunks(t.astype(f32), size) for t in (q, k, v, log_f))
    s_last, o = lax.scan(step, s0.astype(f32), xs)
    return _unchunks(o), s_last


def mlstm_chunked(q, k, v, ig, lf, c0, n0, m0):
    f32 = jnp.float32
    size = MLSTM_CHUNK
    mask = jnp.tril(jnp.ones((size, size), bool))[None, :, :, None]

    def step(carry, inp):
        c, n, m = carry
        qc, kc, vc, ic, fc = inp
        b = jnp.cumsum(fc, axis=1)
        dlog = b[:, :, None, :] - b[:, None, :, :] + ic[:, None, :, :]
        dlog = jnp.where(mask, dlog, -jnp.inf)
        a = b + m[:, None, :]
        mt = jnp.maximum(a, jnp.max(dlog, axis=2))
        dw = jnp.exp(dlog - mt[:, :, None, :])
        aw = jnp.exp(a - mt)
        sc = jnp.einsum('bthk,bshk->btsh', qc, kc) * dw
        num = (aw[..., None] * jnp.einsum('bthk,bhkv->bthv', qc, c)
               + jnp.einsum('btsh,bshv->bthv', sc, vc))
        den = aw * jnp.einsum('bthk,bhk->bth', qc, n) + jnp.sum(sc, axis=2)
        h = num / jnp.maximum(jnp.abs(den), jnp.exp(-mt))[..., None]
        b_end = b[:, -1]
        wlog = b_end[:, None] - b + ic
        m_new = jnp.maximum(b_end + m, jnp.max(wlog, axis=1))
        w = jnp.exp(wlog - m_new[:, None])
        decay = jnp.exp(b_end + m - m_new)
        c_new = decay[..., None, None] * c + jnp.einsum('bsh,bshk,bshv->bhkv', w, kc, vc)
        n_new = decay[..., None] * n + jnp.einsum('bsh,bshk->bhk', w, kc)
        return (c_new, n_new, m_new), h

    xs = tuple(_chunks(t.astype(f32), size) for t in (q, k, v, ig, lf))
    (c_l, n_l, m_l), h = lax.scan(step, (c0.astype(f32), n0.astype(f32), m0.astype(f32)), xs)
    return _unchunks(h), c_l, n_l, m_l


def token_mixing(h, st_h, st_c, st_n, st_m, w_in, gate_b, lb, hgrn_g, mlstm_g, cmlp_g, cmlp_w, cmlp_b):
    f32 = jnp.float32
    bsz, t_len, _ = h.shape
    proj = h @ w_in
    (hq, hi, hgate, hf_fw, hf_bw, mq, mk, mv, mo, mg, cu, cv) = jnp.split(proj, PROJ_SPLITS, axis=-1)

    def heads(t, n_heads):
        return t.reshape(bsz, t_len, n_heads, HEAD_DIM)

    def flip(t):
        return jnp.flip(t, axis=1)

    f_fw = lb[0] + (1.0 - lb[0]) * jax.nn.sigmoid(hf_fw.astype(f32))
    f_bw = lb[1] + (1.0 - lb[1]) * jax.nn.sigmoid(hf_bw.astype(f32))
    q_a = heads(hq.astype(f32), HGRN_HEADS)
    v_a = heads(hi.astype(f32), HGRN_HEADS)
    o_fw, s_fw = hgrn2_chunked(q_a, heads(1.0 - f_fw, HGRN_HEADS), v_a,
                               heads(jnp.log(f_fw), HGRN_HEADS), st_h[:, 0])
    o_bw, s_bw = hgrn2_chunked(flip(q_a), flip(heads(1.0 - f_bw, HGRN_HEADS)), flip(v_a),
                               flip(heads(jnp.log(f_bw), HGRN_HEADS)), st_h[:, 1])
    o_a = (rms_norm(o_fw + flip(o_bw), hgrn_g.reshape(HGRN_HEADS, HEAD_DIM))
           * jax.nn.silu(heads(hgate.astype(f32), MLSTM_HEADS if False else HGRN_HEADS)))
    y_a = o_a.reshape(bsz, t_len, HGRN_WIDTH).astype(h.dtype)

    q_b = heads(mq.astype(f32), MLSTM_HEADS) * HEAD_DIM ** -0.5
    k_b = heads(mk.astype(f32), MLSTM_HEADS)
    v_b = heads(mv.astype(f32), MLSTM_HEADS)
    gates = (mg.astype(f32) + gate_b.astype(f32)).reshape(bsz, t_len, 4, MLSTM_HEADS)
    ig_fw, lf_fw = gates[:, :, 0], jax.nn.log_sigmoid(gates[:, :, 1])
    ig_bw, lf_bw = gates[:, :, 2], jax.nn.log_sigmoid(gates[:, :, 3])
    h_fw, c_fw, n_fw, m_fw = mlstm_chunked(q_b, k_b, v_b, ig_fw, lf_fw,
                                           st_c[:, 0], st_n[:, 0], st_m[:, 0])
    h_bw, c_bw, n_bw, m_bw = mlstm_chunked(flip(q_b), flip(k_b), flip(v_b), flip(ig_bw), flip(lf_bw),
                                           st_c[:, 1], st_n[:, 1], st_m[:, 1])
    o_b = (jax.nn.sigmoid(heads(mo.astype(f32), MLSTM_HEADS))
           * rms_norm(h_fw + flip(h_bw), mlstm_g.reshape(MLSTM_HEADS, HEAD_DIM)))
    y_b = o_b.reshape(bsz, t_len, MLSTM_WIDTH).astype(h.dtype)

    u = jax.nn.gelu(cu)
    vv = rms_norm(jax.nn.gelu(cv), cmlp_g)
    vv = vv.reshape(bsz, t_len // CMLP_CHUNK, CMLP_CHUNK, CMLP_GROUPS, HEAD_DIM)
    mixed = jnp.einsum('gpq,bnqgc->bnpgc', cmlp_w, vv) + cmlp_b.T[None, None, :, :, None]
    y_c = (u * mixed.reshape(bsz, t_len, CMLP_WIDTH)).astype(h.dtype)

    y = jnp.concatenate([y_a, y_b, y_c], axis=-1)
    states = (jnp.stack([s_fw, s_bw], axis=1), jnp.stack([c_fw, c_bw], axis=1),
              jnp.stack([n_fw, n_bw], axis=1), jnp.stack([m_fw, m_bw], axis=1))
    return y, states


def trunk_layer(x, mod, st_h, st_c, st_n, st_m, w_in, gate_b, lb, hgrn_g, mlstm_g, cmlp_g,
                cmlp_w, cmlp_b, w_o, n1, n2, w_gate, w_up, w_down):
    shift1, scale1, gate1, shift2, scale2, gate2 = jnp.split(mod[:, None, :], 6, axis=-1)
    h = rms_norm(x, n1) * (1.0 + scale1) + shift1
    mix, states = token_mixing(h, st_h, st_c, st_n, st_m, w_in, gate_b, lb, hgrn_g, mlstm_g,
                               cmlp_g, cmlp_w, cmlp_b)
    x = x + gate1 * (mix @ w_o)
    h = rms_norm(x, n2) * (1.0 + scale2) + shift2
    x = x + gate2 * ((jax.nn.silu(h @ w_gate) * (h @ w_up)) @ w_down)
    return x, states


def setup_inputs(seed: int = 0) -> dict:
    key = jax.random.key(seed)
    ks = jax.random.split(key, 25)
    nrm = jax.random.normal
    f32 = jnp.float32
    gate_base = jnp.array([0.0, MLSTM_F_BIAS, 0.0, MLSTM_F_BIAS], f32)[None, :, None]
    return {
        'x_prompt': nrm(ks[0], (BATCH, SEQ, D_MODEL), f32),
        'x_sample': nrm(ks[1], (DEC_BATCH, DEC_SEQ, D_MODEL), f32),
        'state_hgrn': 0.5 * nrm(ks[2], (DEC_BATCH, DEPTH, 2, HGRN_HEADS, HEAD_DIM, HEAD_DIM), f32),
        'state_mlstm_c': nrm(ks[3], (DEC_BATCH, DEPTH, 2, MLSTM_HEADS, HEAD_DIM, HEAD_DIM), f32),
        'state_mlstm_n': nrm(ks[4], (DEC_BATCH, DEPTH, 2, MLSTM_HEADS, HEAD_DIM), f32),
        'state_mlstm_m': nrm(ks[5], (DEC_BATCH, DEPTH, 2, MLSTM_HEADS), f32),
        'c': nrm(ks[6], (DEC_BATCH, D_MODEL), f32),
        'c_ctx': nrm(ks[7], (D_MODEL,), f32),
        'w_in': nrm(ks[8], (DEPTH, D_MODEL, PROJ_WIDTH), f32) * D_MODEL ** -0.5,
        'mlstm_gate_b': (gate_base + 0.1 * nrm(ks[9], (DEPTH, 4, MLSTM_HEADS), f32)).reshape(DEPTH, 4 * MLSTM_HEADS),
        'hgrn_lb_logits': 0.1 * nrm(ks[10], (DEPTH + 1, 2, HGRN_WIDTH), f32),
        'hgrn_onorm_g': 1.0 + 0.02 * nrm(ks[11], (DEPTH, HGRN_WIDTH), f32),
        'mlstm_onorm_g': 1.0 + 0.02 * nrm(ks[12], (DEPTH, MLSTM_WIDTH), f32),
        'cmlp_vnorm_g': 1.0 + 0.02 * nrm(ks[13], (DEPTH, CMLP_WIDTH), f32),
        'cmlp_ws': nrm(ks[14], (DEPTH, CMLP_GROUPS, CMLP_CHUNK, CMLP_CHUNK), f32) * CMLP_CHUNK ** -0.5,
        'cmlp_bs': 0.02 * nrm(ks[15], (DEPTH, CMLP_GROUPS, CMLP_CHUNK), f32),
        'w_o': nrm(ks[16], (DEPTH, MIX_WIDTH, D_MODEL), f32) * MIX_WIDTH ** -0.5,
        'norm1_g': 1.0 + 0.02 * nrm(ks[17], (DEPTH, D_MODEL), f32),
        'norm2_g': 1.0 + 0.02 * nrm(ks[18], (DEPTH, D_MODEL), f32),
        'w_ada': 0.5 * nrm(ks[19], (DEPTH, D_MODEL, 6 * D_MODEL), f32) * D_MODEL ** -0.5,
        'b_ada': 0.02 * nrm(ks[20], (DEPTH, 6 * D_MODEL), f32),
        'w_ffn_gate': nrm(ks[21], (DEPTH, D_MODEL, D_FF), f32) * D_MODEL ** -0.5,
        'w_ffn_up': nrm(ks[22], (DEPTH, D_MODEL, D_FF), f32) * D_MODEL ** -0.5,
        'w_ffn_down': nrm(ks[23], (DEPTH, D_FF, D_MODEL), f32) * D_FF ** -0.5,
        'final_g': 1.0 + 0.02 * nrm(ks[24], (D_MODEL,), f32),
    }


def reference(x_prompt, x_sample, state_hgrn, state_mlstm_c, state_mlstm_n, state_mlstm_m, c, c_ctx,
              w_in, mlstm_gate_b, hgrn_lb_logits, hgrn_onorm_g, mlstm_onorm_g, cmlp_vnorm_g,
              cmlp_ws, cmlp_bs, w_o, norm1_g, norm2_g, w_ada, b_ada, w_ffn_gate, w_ffn_up,
              w_ffn_down, final_g):
    f32 = jnp.float32
    lb_all = jnp.cumsum(jax.nn.softmax(hgrn_lb_logits.astype(f32), axis=0), axis=0)
    n_ctx = x_prompt.shape[0]
    zero_h = jnp.zeros((n_ctx, 2, HGRN_HEADS, HEAD_DIM, HEAD_DIM), f32)
    zero_c = jnp.zeros((n_ctx, 2, MLSTM_HEADS, HEAD_DIM, HEAD_DIM), f32)
    zero_n = jnp.zeros((n_ctx, 2, MLSTM_HEADS, HEAD_DIM), f32)
    zero_m = jnp.zeros((n_ctx, 2, MLSTM_HEADS), f32)
    xp, xs = x_prompt, x_sample
    new_h, new_c, new_n, new_m = [], [], [], []
    for l in range(DEPTH):
        mod_ctx = (jax.nn.silu(c_ctx) @ w_ada[l] + b_ada[l])[None, :]
        mod_lat = jax.nn.silu(c) @ w_ada[l] + b_ada[l]
        layer = functools.partial(
            trunk_layer, w_in=w_in[l], gate_b=mlstm_gate_b[l], lb=lb_all[l],
            hgrn_g=hgrn_onorm_g[l], mlstm_g=mlstm_onorm_g[l], cmlp_g=cmlp_vnorm_g[l],
            cmlp_w=cmlp_ws[l], cmlp_b=cmlp_bs[l], w_o=w_o[l], n1=norm1_g[l], n2=norm2_g[l],
            w_gate=w_ffn_gate[l], w_up=w_ffn_up[l], w_down=w_ffn_down[l])
        xp, (sh, sc, sn, sm) = layer(xp, mod_ctx, zero_h, zero_c, zero_n, zero_m)
        new_h.append(sh)
        new_c.append(sc)
        new_n.append(sn)
        new_m.append(sm)
        xs, _ = layer(xs, mod_lat, state_hgrn[:, l], state_mlstm_c[:, l],
                      state_mlstm_n[:, l], state_mlstm_m[:, l])
    y_prompt = rms_norm(xp, final_g)
    y_sample = rms_norm(xs, final_g)
    new_hgrn = jnp.stack(new_h, axis=1)
    new_mlstm_c = jnp.stack(new_c, axis=1)
    new_mlstm_n = jnp.stack(new_n, axis=1)
    new_mlstm_m = jnp.stack(new_m, axis=1)
    return (y_prompt, y_sample, new_hgrn, new_mlstm_c, new_mlstm_n, new_mlstm_m)
```

```python
import functools

import jax
import jax.numpy as jnp
from jax import lax
from jax.experimental import pallas as pl
from jax.experimental.pallas import tpu as pltpu

F32 = jnp.float32
BF16 = jnp.bfloat16

HEAD_DIM = 128
HGRN_CHUNK = 32
MLSTM_CHUNK = 64
CMLP_CHUNK = 128
EPS = 1e-6
SEG = 256
GATE_LANES = 256
V7X_VMEM_BYTES = 64 * 1024 * 1024
VMEM_CAP_BYTES = V7X_VMEM_BYTES - 6 * 1024 * 1024


def _vmem_limit(block_bytes, temp_bytes=0):
    need = 2 * block_bytes + 2 * temp_bytes + 4 * 1024 * 1024
    return int(min(max(need, 16 * 1024 * 1024), VMEM_CAP_BYTES))


def _pick(n, candidates):
    for c in candidates:
        if c <= n and n % c == 0:
            return c
    return n


def _dot(a, b):
    return jnp.dot(a, b, preferred_element_type=F32)


def _dot_nt(a, b):
    return lax.dot_general(a, b, (((1,), (1,)), ((), ())), preferred_element_type=F32)


def _dot_tn(a, b):
    return lax.dot_general(a, b, (((0,), (0,)), ((), ())), preferred_element_type=F32)


def _sigmoid(x):
    return 1.0 / (1.0 + jnp.exp(-x))


def _split3(x):
    x1 = x.astype(BF16)
    r1 = x - x1.astype(F32)
    x2 = r1.astype(BF16)
    x3 = (r1 - x2.astype(F32)).astype(BF16)
    return x1, x2, x3


def _chunk_masks(chunk, reverse):
    r = lax.broadcasted_iota(jnp.int32, (SEG, SEG), 0)
    c = lax.broadcasted_iota(jnp.int32, (SEG, SEG), 1)
    same = (r // chunk) == (c // chunk)
    causal = same & ((c >= r) if reverse else (c <= r))
    return same, causal


def _chunk_sums(x, same, causal):
    w = x.shape[1]
    one_hot = lambda mask: jnp.where(mask, 1.0, 0.0).astype(BF16)
    lhs = jnp.concatenate([one_hot(causal), one_hot(same)], axis=0)
    pieces = jnp.concatenate(_split3(x), axis=1)
    res = _dot(lhs, pieces)
    res = res[:, :w] + res[:, w:2 * w] + res[:, 2 * w:]
    return res[:SEG], res[SEG:]


def _ada_kernel(c_ref, w_ref, b_ref, o_ref):
    c = c_ref[...]
    a = (c * _sigmoid(c)).astype(BF16)
    o_ref[...] = _dot(a, w_ref[...].astype(BF16)) + b_ref[...]


def _ada_mod(c_all, w_ada, b_ada):
    depth, d, n6 = w_ada.shape
    rows = c_all.shape[0]
    tn = _pick(n6, (512, 256, 128))
    blk = rows * d * 4 + d * tn * 4 + tn * 4 + rows * tn * 4
    return pl.pallas_call(
        _ada_kernel,
        out_shape=jax.ShapeDtypeStruct((depth, rows, n6), F32),
        grid=(depth, n6 // tn),
        in_specs=[
            pl.BlockSpec((rows, d), lambda l, j: (0, 0)),
            pl.BlockSpec((None, d, tn), lambda l, j: (l, 0, j)),
            pl.BlockSpec((None, 1, tn), lambda l, j: (l, 0, j)),
        ],
        out_specs=pl.BlockSpec((None, rows, tn), lambda l, j: (l, 0, j)),
        compiler_params=pltpu.CompilerParams(
            dimension_semantics=("arbitrary", "arbitrary"),
            vmem_limit_bytes=_vmem_limit(blk, d * tn * 2)),
        name="ada_mod",
    )(c_all, w_ada, b_ada.reshape(depth, 1, n6))


def _mod_row(first_row, ctx_rows, lat_len):
    return jnp.where(first_row < ctx_rows, 0, 1 + (first_row - ctx_rows) // lat_len)


def _normmod_kernel(x_ref, g_ref, mod_ref, o_ref, *, shift_idx, scale_idx):
    x = x_ref[...]
    y = x * lax.rsqrt(jnp.mean(x * x, axis=-1, keepdims=True) + EPS) * g_ref[...]
    scale = mod_ref[scale_idx:scale_idx + 1, :]
    shift = mod_ref[shift_idx:shift_idx + 1, :]
    o_ref[...] = (y * (1.0 + scale) + shift).astype(o_ref.dtype)


def _normmod(x, gain, mod, shift_idx, scale_idx, ctx_rows, lat_len):
    n, d = x.shape
    rb = SEG
    blk = rb * d * 4 + d * 4 + 6 * d * 4 + rb * d * 2
    return pl.pallas_call(
        functools.partial(_normmod_kernel, shift_idx=shift_idx, scale_idx=scale_idx),
        out_shape=jax.ShapeDtypeStruct((n, d), BF16),
        grid=(n // rb,),
        in_specs=[
            pl.BlockSpec((rb, d), lambda i: (i, 0)),
            pl.BlockSpec((1, d), lambda i: (0, 0)),
            pl.BlockSpec((None, 6, d), lambda i: (_mod_row(i * rb, ctx_rows, lat_len), 0, 0)),
        ],
        out_specs=pl.BlockSpec((rb, d), lambda i: (i, 0)),
        compiler_params=pltpu.CompilerParams(
            dimension_semantics=("arbitrary",), vmem_limit_bytes=_vmem_limit(blk, 2 * rb * d * 4)),
        name="normmod",
    )(x, gain.reshape(1, d), mod)


def _final_norm_kernel(x_ref, g_ref, o_ref):
    x = x_ref[...]
    o_ref[...] = x * lax.rsqrt(jnp.mean(x * x, axis=-1, keepdims=True) + EPS) * g_ref[...]


def _final_norm(x, gain, row0, rows):
    d = x.shape[1]
    rb = SEG
    blk = 2 * rb * d * 4 + d * 4
    return pl.pallas_call(
        _final_norm_kernel,
        out_shape=jax.ShapeDtypeStruct((rows, d), F32),
        grid=(rows // rb,),
        in_specs=[
            pl.BlockSpec((rb, d), lambda i: (row0 // rb + i, 0)),
            pl.BlockSpec((1, d), lambda i: (0, 0)),
        ],
        out_specs=pl.BlockSpec((rb, d), lambda i: (i, 0)),
        compiler_params=pltpu.CompilerParams(
            dimension_semantics=("arbitrary",), vmem_limit_bytes=_vmem_limit(blk, rb * d * 4)),
        name="final_norm",
    )(x, gain.reshape(1, d))


def _mm_kernel(x_ref, w_ref, o_ref):
    o_ref[...] = _dot(x_ref[...], w_ref[...]).astype(o_ref.dtype)


def _mm_plain(x, w, layer, bm, bn, out_dtype):
    m, k = x.shape
    n = w.shape[2]
    osz = jnp.dtype(out_dtype).itemsize
    blk = bm * k * 2 + k * bn * 2 + bm * bn * osz
    return pl.pallas_call(
        _mm_kernel,
        out_shape=jax.ShapeDtypeStruct((m, n), out_dtype),
        grid=(m // bm, n // bn),
        in_specs=[
            pl.BlockSpec((bm, k), lambda i, j: (i, 0)),
            pl.BlockSpec((None, k, bn), lambda i, j: (layer, 0, j)),
        ],
        out_specs=pl.BlockSpec((bm, bn), lambda i, j: (i, j)),
        compiler_params=pltpu.CompilerParams(
            dimension_semantics=("arbitrary", "arbitrary"),
            vmem_limit_bytes=_vmem_limit(blk, bm * bn * 4)),
        name="proj_matmul",
    )(x, w)


def _swiglu_kernel(x_ref, wg_ref, wu_ref, o_ref):
    x = x_ref[...]
    g = _dot(x, wg_ref[...])
    u = _dot(x, wu_ref[...])
    o_ref[...] = (g * _sigmoid(g) * u).astype(o_ref.dtype)


def _mm_swiglu(x, wg, wu, layer, bm, bn):
    m, k = x.shape
    n = wg.shape[2]
    blk = bm * k * 2 + 2 * k * bn * 2 + bm * bn * 2
    return pl.pallas_call(
        _swiglu_kernel,
        out_shape=jax.ShapeDtypeStruct((m, n), BF16),
        grid=(m // bm, n // bn),
        in_specs=[
            pl.BlockSpec((bm, k), lambda i, j: (i, 0)),
            pl.BlockSpec((None, k, bn), lambda i, j: (layer, 0, j)),
            pl.BlockSpec((None, k, bn), lambda i, j: (layer, 0, j)),
        ],
        out_specs=pl.BlockSpec((bm, bn), lambda i, j: (i, j)),
        compiler_params=pltpu.CompilerParams(
            dimension_semantics=("arbitrary", "arbitrary"),
            vmem_limit_bytes=_vmem_limit(blk, 3 * bm * bn * 4)),
        name="ffn_swiglu",
    )(x, wg, wu)


def _mm_residual_kernel(x_ref, w_ref, r_ref, mod_ref, o_ref, acc_ref, *, gate_idx, nk):
    kk = pl.program_id(2)
    p = _dot(x_ref[...], w_ref[...])

    @pl.when(kk == 0)
    def _():
        acc_ref[...] = p

    @pl.when(kk > 0)
    def _():
        acc_ref[...] += p

    @pl.when(kk == nk - 1)
    def _():
        gate = mod_ref[gate_idx:gate_idx + 1, :]
        o_ref[...] = r_ref[...] + gate * acc_ref[...]


def _mm_residual(x, w, layer, res, mod, gate_idx, bm, bn, bk, ctx_rows, lat_len):
    m, k = x.shape
    n = w.shape[2]
    nk = k // bk
    blk = bm * bk * 2 + bk * bn * 2 + 2 * bm * bn * 4 + 6 * bn * 4
    return pl.pallas_call(
        functools.partial(_mm_residual_kernel, gate_idx=gate_idx, nk=nk),
        out_shape=jax.ShapeDtypeStruct((m, n), F32),
        grid=(m // bm, n // bn, nk),
        in_specs=[
            pl.BlockSpec((bm, bk), lambda i, j, kk: (i, kk)),
            pl.BlockSpec((None, bk, bn), lambda i, j, kk: (layer, kk, j)),
            pl.BlockSpec((bm, bn), lambda i, j, kk: (i, j)),
            pl.BlockSpec((None, 6, bn), lambda i, j, kk: (_mod_row(i * bm, ctx_rows, lat_len), 0, j)),
        ],
        out_specs=pl.BlockSpec((bm, bn), lambda i, j, kk: (i, j)),
        scratch_shapes=[pltpu.VMEM((bm, bn), F32)],
        compiler_params=pltpu.CompilerParams(
            dimension_semantics=("arbitrary", "arbitrary", "arbitrary"),
            vmem_limit_bytes=_vmem_limit(blk, 2 * bm * bn * 4)),
        name="residual_matmul",
    )(x, w, res, mod)


class _Segs:
    def __init__(self, ctx_seqs, ctx_len, lat_seqs, lat_len):
        self.ctx_sps = ctx_len // SEG
        self.lat_sps = lat_len // SEG
        self.ctx_seqs = ctx_seqs
        self.lat_seqs = lat_seqs
        self.ctx_segs = ctx_seqs * self.ctx_sps
        self.nseg = self.ctx_segs + lat_seqs * self.lat_sps

    def seg_of_step(self, i, reverse):
        return (self.nseg - 1 - i) if reverse else i

    def position(self, seg):
        is_ctx = seg < self.ctx_segs
        pos = jnp.where(is_ctx, seg % self.ctx_sps, (seg - self.ctx_segs) % self.lat_sps)
        sps = jnp.where(is_ctx, self.ctx_sps, self.lat_sps)
        return is_ctx, pos, sps

    def lat_seq(self, seg):
        return jnp.clip((seg - self.ctx_segs) // self.lat_sps, 0, self.lat_seqs - 1)

    def ctx_seq(self, seg):
        return jnp.clip(seg // self.ctx_sps, 0, self.ctx_seqs - 1)


def _seq_edges(segs, seg, reverse):
    is_ctx, pos, sps = segs.position(seg)
    first = pos == (sps - 1 if reverse else 0)
    last = pos == (0 if reverse else sps - 1)
    return is_ctx, first, last


def _hgrn_kernel(*refs, segs, heads, reverse):
    if reverse:
        q_ref, v_ref, f_ref, lb_ref, s0_ref, o_ref, sfin_ref, st_ref = refs
    else:
        (q_ref, v_ref, f_ref, lb_ref, s0_ref, gate_ref, obw_ref, gain_ref, _y_in,
         o_ref, sfin_ref, st_ref) = refs
    seg = segs.seg_of_step(pl.program_id(0), reverse)
    is_ctx, first, last = _seq_edges(segs, seg, reverse)

    @pl.when(first & is_ctx)
    def _():
        st_ref[...] = jnp.zeros_like(st_ref)

    @pl.when(first & jnp.logical_not(is_ctx))
    def _():
        for h in range(heads):
            st_ref[h] = s0_ref[h].T

    same, causal = _chunk_masks(HGRN_CHUNK, reverse)
    n_chunks = SEG // HGRN_CHUNK
    order = range(n_chunks - 1, -1, -1) if reverse else range(n_chunks)

    for h in range(heads):
        sl = slice(h * HEAD_DIM, (h + 1) * HEAD_DIM)
        lb = lb_ref[:, sl]
        f = lb + (1.0 - lb) * _sigmoid(f_ref[:, sl])
        k = 1.0 - f
        b, bend = _chunk_sums(jnp.log(f), same, causal)
        qd = (q_ref[:, sl] * jnp.exp(b)).astype(BF16)
        ki = (k * jnp.exp(-b)).astype(BF16)
        ke = (k * jnp.exp(bend - b)).astype(BF16)
        vb = v_ref[:, sl].astype(BF16)
        decay = jnp.exp(bend)
        att = jnp.where(causal, _dot_nt(qd, ki), 0.0).astype(BF16)
        o_intra = _dot(att, vb)
        st = st_ref[h]
        o_state = [None] * n_chunks
        for c in order:
            rs = slice(c * HGRN_CHUNK, (c + 1) * HGRN_CHUNK)
            o_state[c] = _dot_nt(qd[rs], st.astype(BF16))
            st = st * decay[c * HGRN_CHUNK:c * HGRN_CHUNK + 1, :] + _dot_tn(vb[rs], ke[rs])
        st_ref[h] = st
        o = o_intra + jnp.concatenate(o_state, axis=0)
        if reverse:
            o_ref[:, sl] = o
        else:
            tot = o + obw_ref[:, sl]
            y = tot * lax.rsqrt(jnp.mean(tot * tot, axis=-1, keepdims=True) + EPS) * gain_ref[:, sl]
            g = gate_ref[:, sl]
            o_ref[:, sl] = (y * (g * _sigmoid(g))).astype(o_ref.dtype)

    @pl.when(last & is_ctx)
    def _():
        for h in range(heads):
            sfin_ref[h] = st_ref[h].T


def _hgrn_pass(proj, lb, state, layer, segs, reverse, *, gain=None, obw=None, y=None):
    n = proj.shape[0]
    w = proj.shape[1] // 5
    heads = w // HEAD_DIM
    d = 1 if reverse else 0

    def seg_idx(i):
        return segs.seg_of_step(i, reverse)

    col = lambda cb: pl.BlockSpec((SEG, w), lambda i: (seg_idx(i), cb))
    in_specs = [
        col(0), col(1), col(4 if reverse else 3),
        pl.BlockSpec((None, 1, w), lambda i: (d, 0, 0)),
        pl.BlockSpec((None, None, None, heads, HEAD_DIM, HEAD_DIM),
                     lambda i: (segs.lat_seq(seg_idx(i)), layer, d, 0, 0, 0)),
    ]
    args = [proj, proj, proj, lb, state]
    sfin_shape = jax.ShapeDtypeStruct((segs.ctx_seqs, heads, HEAD_DIM, HEAD_DIM), F32)
    sfin_spec = pl.BlockSpec((None, heads, HEAD_DIM, HEAD_DIM),
                             lambda i: (segs.ctx_seq(seg_idx(i)), 0, 0, 0))
    blk = 3 * SEG * w * 4 + w * 4 + 2 * heads * HEAD_DIM * HEAD_DIM * 4
    if reverse:
        out_shape = [jax.ShapeDtypeStruct((n, w), F32), sfin_shape]
        out_specs = [pl.BlockSpec((SEG, w), lambda i: (seg_idx(i), 0)), sfin_spec]
        aliases = {}
        blk += SEG * w * 4
    else:
        in_specs += [col(2), pl.BlockSpec((SEG, w), lambda i: (i, 0)),
                     pl.BlockSpec((1, w), lambda i: (0, 0)),
                     pl.BlockSpec(memory_space=pl.ANY)]
        args += [proj, obw, gain.reshape(1, w), y]
        out_shape = [jax.ShapeDtypeStruct(y.shape, y.dtype), sfin_shape]
        out_specs = [pl.BlockSpec((SEG, w), lambda i: (i, 0)), sfin_spec]
        aliases = {8: 0}
        blk += 2 * SEG * w * 4 + SEG * w * 2 + w * 4
    return pl.pallas_call(
        functools.partial(_hgrn_kernel, segs=segs, heads=heads, reverse=reverse),
        out_shape=out_shape,
        grid=(segs.nseg,),
        in_specs=in_specs,
        out_specs=out_specs,
        scratch_shapes=[pltpu.VMEM((heads, HEAD_DIM, HEAD_DIM), F32)],
        input_output_aliases=aliases,
        compiler_params=pltpu.CompilerParams(
            dimension_semantics=("arbitrary",),
            vmem_limit_bytes=_vmem_limit(blk, 16 * 1024 * 1024)),
        name="hgrn_bwd" if reverse else "hgrn_fwd",
    )(*args)


def _running_max(x, chunk, reverse):
    rows = x.shape[0]
    pos = lax.broadcasted_iota(jnp.int32, x.shape, 0) % chunk
    step = 1
    while step < chunk:
        if reverse:
            shifted = pltpu.roll(x, rows - step, axis=0)
            ok = pos < chunk - step
        else:
            shifted = pltpu.roll(x, step, axis=0)
            ok = pos >= step
        x = jnp.maximum(x, jnp.where(ok, shifted, -jnp.inf))
        step *= 2
    return x


def _mlstm_kernel(*refs, segs, heads, reverse):
    if reverse:
        (q_ref, k_ref, v_ref, gt_ref, gb_ref, c0_ref, n0_ref, m0_ref,
         o_ref, cfin_ref, nfin_ref, mfin_ref, ct_ref, n_ref, m_ref) = refs
    else:
        (q_ref, k_ref, v_ref, gt_ref, gb_ref, c0_ref, n0_ref, m0_ref, og_ref, hbw_ref, gain_ref, _y_in,
         o_ref, cfin_ref, nfin_ref, mfin_ref, ct_ref, n_ref, m_ref) = refs
    seg = segs.seg_of_step(pl.program_id(0), reverse)
    is_ctx, first, last = _seq_edges(segs, seg, reverse)
    lane0 = heads if reverse else 0

    @pl.when(first & is_ctx)
    def _():
        ct_ref[...] = jnp.zeros_like(ct_ref)
        n_ref[...] = jnp.zeros_like(n_ref)
        m_ref[...] = jnp.zeros_like(m_ref)

    @pl.when(first & jnp.logical_not(is_ctx))
    def _():
        for h in range(heads):
            ct_ref[h] = c0_ref[h].T
        n_ref[...] = n0_ref[...]
        m_ref[...] = m0_ref[...]

    half = GATE_LANES // 2
    gates = gt_ref[...] + gb_ref[...]
    ig = gates[:, :half]
    fpre = gates[:, half:]
    lf = jnp.minimum(fpre, 0.0) - jnp.log1p(jnp.exp(-jnp.abs(fpre)))
    same, causal = _chunk_masks(MLSTM_CHUNK, reverse)
    b, _ = _chunk_sums(lf, same, causal)
    u = ig - b
    mrun = _running_max(u, MLSTM_CHUNK, reverse)
    n_chunks = SEG // MLSTM_CHUNK
    order = list(range(n_chunks - 1, -1, -1) if reverse else range(n_chunks))
    m_cur = m_ref[...]
    mu_parts, mul_parts, mcur_parts, dec_rows = ([None] * n_chunks for _ in range(4))
    for c in order:
        r0 = c * MLSTM_CHUNK
        end = r0 if reverse else r0 + MLSTM_CHUNK - 1
        mu_c = jnp.maximum(m_cur, mrun[r0:r0 + MLSTM_CHUNK])
        mu_end = mu_c[end - r0:end - r0 + 1]
        mu_parts[c] = mu_c
        mul_parts[c] = jnp.broadcast_to(mu_end, mu_c.shape)
        mcur_parts[c] = jnp.broadcast_to(m_cur, mu_c.shape)
        dec_rows[c] = jnp.exp(m_cur - mu_end)
        m_cur = b[end:end + 1] + mu_end
    m_ref[...] = m_cur
    mu = jnp.concatenate(mu_parts, axis=0)
    aw_all = jnp.exp(jnp.concatenate(mcur_parts, axis=0) - mu)
    emt_all = jnp.exp(-(b + mu))
    w_all = jnp.exp(u - jnp.concatenate(mul_parts, axis=0))
    u_t = u.T

    scale = HEAD_DIM ** -0.5
    for h in range(heads):
        sl = slice(h * HEAD_DIM, (h + 1) * HEAD_DIM)
        j = lane0 + h
        qs = (q_ref[:, sl] * scale).astype(BF16)
        kf = k_ref[:, sl]
        kb = kf.astype(BF16)
        vb = v_ref[:, sl].astype(BF16)
        dmat = jnp.where(causal, jnp.exp(u_t[j:j + 1, :] - mu[:, j:j + 1]), 0.0)
        sc = _dot_nt(qs, kb) * dmat
        rowsum = jnp.sum(sc, axis=1, keepdims=True)
        num_intra = _dot(sc.astype(BF16), vb)
        aw = aw_all[:, j:j + 1]
        emt = emt_all[:, j:j + 1]
        wcol = w_all[:, j:j + 1]
        wk = (wcol * kf).astype(BF16)
        wn = wcol.astype(BF16).astype(F32) * kb.astype(F32)
        ct = ct_ref[h]
        nv = n_ref[h:h + 1, :]
        h_parts = [None] * n_chunks
        for c in order:
            rs = slice(c * MLSTM_CHUNK, (c + 1) * MLSTM_CHUNK)
            rhs = jnp.concatenate([ct, nv, jnp.zeros((15, HEAD_DIM), F32)], axis=0).astype(BF16)
            qc = _dot_nt(qs[rs], rhs)
            num = aw[rs] * qc[:, :HEAD_DIM] + num_intra[rs]
            den = aw[rs] * qc[:, HEAD_DIM:HEAD_DIM + 1] + rowsum[rs]
            h_parts[c] = num / jnp.maximum(jnp.abs(den), emt[rs])
            dec = dec_rows[c][:, j:j + 1]
            ct = dec * ct + _dot_tn(vb[rs], wk[rs])
            nv = dec * nv + jnp.sum(wn[rs], axis=0, keepdims=True)
        ct_ref[h] = ct
        n_ref[h:h + 1, :] = nv
        hs = jnp.concatenate(h_parts, axis=0)
        if reverse:
            o_ref[:, sl] = hs
        else:
            tot = hs + hbw_ref[:, sl]
            y = tot * lax.rsqrt(jnp.mean(tot * tot, axis=-1, keepdims=True) + EPS) * gain_ref[:, sl]
            o_ref[:, sl] = (_sigmoid(og_ref[:, sl]) * y).astype(o_ref.dtype)

    @pl.when(last & is_ctx)
    def _():
        for h in range(heads):
            cfin_ref[h] = ct_ref[h].T
        nfin_ref[...] = n_ref[...]
        mfin_ref[...] = m_ref[...]


def _mlstm_pass(proj, gate_b, c0, n0, m0, layer, segs, reverse, *, gain=None, hbw=None, y=None):
    n = proj.shape[0]
    w = (proj.shape[1] - GATE_LANES) // 4
    heads = w // HEAD_DIM
    d = 1 if reverse else 0

    def seg_idx(i):
        return segs.seg_of_step(i, reverse)

    def lat(i):
        return segs.lat_seq(seg_idx(i))

    col = lambda cb: pl.BlockSpec((SEG, w), lambda i: (seg_idx(i), cb))
    in_specs = [
        col(0), col(1), col(2),
        pl.BlockSpec((SEG, GATE_LANES), lambda i: (seg_idx(i), 4 * w // GATE_LANES)),
        pl.BlockSpec((1, GATE_LANES), lambda i: (0, 0)),
        pl.BlockSpec((None, None, None, heads, HEAD_DIM, HEAD_DIM), lambda i: (lat(i), layer, d, 0, 0, 0)),
        pl.BlockSpec((None, None, None, heads, HEAD_DIM), lambda i: (lat(i), layer, d, 0, 0)),
        pl.BlockSpec((None, None, 1, HEAD_DIM), lambda i: (lat(i), layer, 0, 0)),
    ]
    args = [proj, proj, proj, proj, gate_b, c0, n0, m0]
    fin_shapes = [
        jax.ShapeDtypeStruct((segs.ctx_seqs, heads, HEAD_DIM, HEAD_DIM), F32),
        jax.ShapeDtypeStruct((segs.ctx_seqs, heads, HEAD_DIM), F32),
        jax.ShapeDtypeStruct((segs.ctx_seqs, 1, HEAD_DIM), F32),
    ]

    def ctx(i):
        return segs.ctx_seq(seg_idx(i))

    fin_specs = [
        pl.BlockSpec((None, heads, HEAD_DIM, HEAD_DIM), lambda i: (ctx(i), 0, 0, 0)),
        pl.BlockSpec((None, heads, HEAD_DIM), lambda i: (ctx(i), 0, 0)),
        pl.BlockSpec((None, 1, HEAD_DIM), lambda i: (ctx(i), 0, 0)),
    ]
    blk = 3 * SEG * w * 4 + SEG * GATE_LANES * 4 + 2 * heads * HEAD_DIM * HEAD_DIM * 4
    if reverse:
        out_shape = [jax.ShapeDtypeStruct((n, w), F32)] + fin_shapes
        out_specs = [pl.BlockSpec((SEG, w), lambda i: (seg_idx(i), 0))] + fin_specs
        aliases = {}
        blk += SEG * w * 4
    else:
        in_specs += [col(3), pl.BlockSpec((SEG, w), lambda i: (i, 0)),
                     pl.BlockSpec((1, w), lambda i: (0, 0)),
                     pl.BlockSpec(memory_space=pl.ANY)]
        args += [proj, hbw, gain.reshape(1, w), y]
        out_shape = [jax.ShapeDtypeStruct(y.shape, y.dtype)] + fin_shapes
        out_specs = [pl.BlockSpec((SEG, w), lambda i: (i, 1))] + fin_specs
        aliases = {11: 0}
        blk += 2 * SEG * w * 4 + SEG * w * 2
    return pl.pallas_call(
        functools.partial(_mlstm_kernel, segs=segs, heads=heads, reverse=reverse),
        out_shape=out_shape,
        grid=(segs.nseg,),
        in_specs=in_specs,
        out_specs=out_specs,
        scratch_shapes=[pltpu.VMEM((heads, HEAD_DIM, HEAD_DIM), F32),
                        pltpu.VMEM((heads, HEAD_DIM), F32),
                        pltpu.VMEM((1, HEAD_DIM), F32)],
        input_output_aliases=aliases,
        compiler_params=pltpu.CompilerParams(
            dimension_semantics=("arbitrary",),
            vmem_limit_bytes=_vmem_limit(blk, 16 * 1024 * 1024)),
        name="mlstm_bwd" if reverse else "mlstm_fwd",
    )(*args)


def _gelu(x):
    return 0.5 * x * (1.0 + jnp.tanh(0.7978845608028654 * (x + 0.044715 * (x * x * x))))


def _cmlp_kernel(u_ref, v_ref, g_ref, w_ref, b_ref, _y_in, o_ref, *, groups):
    gv = _gelu(v_ref[...])
    vv = (gv * lax.rsqrt(jnp.mean(gv * gv, axis=-1, keepdims=True) + EPS) * g_ref[...]).astype(BF16)
    for g in range(groups):
        sl = slice(g * HEAD_DIM, (g + 1) * HEAD_DIM)
        wg = w_ref[g].astype(BF16)
        bias = b_ref[:, g:g + 1]
        for c in range(SEG // CMLP_CHUNK):
            rs = slice(c * CMLP_CHUNK, (c + 1) * CMLP_CHUNK)
            mixed = _dot(wg, vv[rs, sl]) + bias
            o_ref[rs, sl] = (_gelu(u_ref[rs, sl]) * mixed).astype(o_ref.dtype)


def _cmlp(proj, gain, ws, bs_t, layer, y, col_block):
    n = proj.shape[0]
    w = proj.shape[1] // 2
    groups = w // HEAD_DIM
    blk = 2 * SEG * w * 4 + w * 4 + groups * CMLP_CHUNK * CMLP_CHUNK * 4 + CMLP_CHUNK * 128 * 4 + SEG * w * 2
    return pl.pallas_call(
        functools.partial(_cmlp_kernel, groups=groups),
        out_shape=jax.ShapeDtypeStruct(y.shape, y.dtype),
        grid=(n // SEG,),
        in_specs=[
            pl.BlockSpec((SEG, w), lambda i: (i, 0)),
            pl.BlockSpec((SEG, w), lambda i: (i, 1)),
            pl.BlockSpec((1, w), lambda i: (0, 0)),
            pl.BlockSpec((None, groups, CMLP_CHUNK, CMLP_CHUNK), lambda i: (layer, 0, 0, 0)),
            pl.BlockSpec((None, CMLP_CHUNK, groups), lambda i: (layer, 0, 0)),
            pl.BlockSpec(memory_space=pl.ANY),
        ],
        out_specs=pl.BlockSpec((SEG, w), lambda i: (i, col_block)),
        input_output_aliases={5: 0},
        compiler_params=pltpu.CompilerParams(
            dimension_semantics=("arbitrary",), vmem_limit_bytes=_vmem_limit(blk, 8 * 1024 * 1024)),
        name="cmlp",
    )(proj, proj, gain.reshape(1, w), ws, bs_t, y)


def kernel(x_prompt, x_sample, state_hgrn, state_mlstm_c, state_mlstm_n, state_mlstm_m, c, c_ctx, w_in, mlstm_gate_b, hgrn_lb_logits, hgrn_onorm_g, mlstm_onorm_g, cmlp_vnorm_g, cmlp_ws, cmlp_bs, w_o, norm1_g, norm2_g, w_ada, b_ada, w_ffn_gate, w_ffn_up, w_ffn_down, final_g):
    batch, seq, d = x_prompt.shape
    dec_batch, dec_seq, _ = x_sample.shape
    depth = w_in.shape[0]
    wh = hgrn_onorm_g.shape[1]
    wm = mlstm_onorm_g.shape[1]
    wc = cmlp_vnorm_g.shape[1]
    hh, hm = wh // HEAD_DIM, wm // HEAD_DIM
    d_ff = w_ffn_gate.shape[2]
    assert seq % SEG == 0 and dec_seq % SEG == 0 and wh == wm
    assert 2 * hm <= GATE_LANES // 2 and (wh + wm) % wc == 0 and wh + wm + wc == d
    ctx_rows = batch * seq
    n = ctx_rows + dec_batch * dec_seq
    segs = _Segs(batch, seq, dec_batch, dec_seq)

    o_m = 5 * wh
    o_g = o_m + 4 * wm
    o_c = o_g + 4 * hm
    w_h = w_in[:, :, :o_m].astype(BF16)
    wg_cols = w_in[:, :, o_g:o_c]
    zpad = jnp.zeros((depth, d, GATE_LANES // 2 - 2 * hm), w_in.dtype)
    w_m = jnp.concatenate(
        [w_in[:, :, o_m:o_g], wg_cols[:, :, 0:hm], wg_cols[:, :, 2 * hm:3 * hm], zpad,
         wg_cols[:, :, hm:2 * hm], wg_cols[:, :, 3 * hm:4 * hm], zpad], axis=2).astype(BF16)
    w_c = w_in[:, :, o_c:].astype(BF16)
    bpad = jnp.zeros((depth, GATE_LANES // 2 - 2 * hm), F32)
    gb = mlstm_gate_b.astype(F32)
    gate_b = jnp.concatenate(
        [gb[:, 0:hm], gb[:, 2 * hm:3 * hm], bpad, gb[:, hm:2 * hm], gb[:, 3 * hm:4 * hm], bpad], axis=1)
    w_o_b = w_o.astype(BF16)
    ff_pad = (-d_ff) % 1024
    w_gate_b = jnp.pad(w_ffn_gate.astype(BF16), ((0, 0), (0, 0), (0, ff_pad)))
    w_up_b = jnp.pad(w_ffn_up.astype(BF16), ((0, 0), (0, 0), (0, ff_pad)))
    w_down_b = jnp.pad(w_ffn_down.astype(BF16), ((0, 0), (0, ff_pad), (0, 0)))
    d_ffp = d_ff + ff_pad

    lb_all = jnp.cumsum(jax.nn.softmax(hgrn_lb_logits.astype(F32), axis=0), axis=0)
    m0_rows = jnp.concatenate(
        [state_mlstm_m[:, :, 0, :], state_mlstm_m[:, :, 1, :],
         jnp.zeros((dec_batch, depth, HEAD_DIM - 2 * hm), F32)], axis=-1)[:, :, None, :]
    cmlp_bs_t = jnp.swapaxes(cmlp_bs, 1, 2)

    ada_rows = 16
    c_all = jnp.concatenate(
        [c_ctx[None, :], c, jnp.zeros((ada_rows - 1 - dec_batch, d), F32)], axis=0)
    mod_all = _ada_mod(c_all, w_ada, b_ada)[:, :1 + dec_batch].reshape(depth, 1 + dec_batch, 6, d)

    bm = _pick(min(ctx_rows, dec_seq), (1024, 512, 256))
    x = jnp.concatenate([x_prompt.reshape(ctx_rows, d), x_sample.reshape(dec_batch * dec_seq, d)], axis=0)
    new_h, new_c, new_n, new_m = [], [], [], []
    for l in range(depth):
        mod = mod_all[l]
        h = _normmod(x, norm1_g[l], mod, 0, 1, ctx_rows, dec_seq)
        proj_h = _mm_plain(h, w_h, l, bm, _pick(5 * wh, (768, 640, 512, 384, 256, 128)), F32)
        proj_m = _mm_plain(h, w_m, l, bm, _pick(4 * wm + GATE_LANES, (1280, 1024, 896, 640, 512, 256)), F32)
        proj_c = _mm_plain(h, w_c, l, bm, _pick(2 * wc, (1024, 512, 256)), F32)

        y = jnp.zeros((n, d), BF16)
        obw, sh_b = _hgrn_pass(proj_h, lb_all[l][:, None, :], state_hgrn, l, segs, True)
        y, sh_f = _hgrn_pass(proj_h, lb_all[l][:, None, :], state_hgrn, l, segs, False,
                             gain=hgrn_onorm_g[l], obw=obw, y=y)
        hbw, sc_b, sn_b, sm_b = _mlstm_pass(proj_m, gate_b[l][None, :], state_mlstm_c, state_mlstm_n,
                                            m0_rows, l, segs, True)
        y, sc_f, sn_f, sm_f = _mlstm_pass(proj_m, gate_b[l][None, :], state_mlstm_c, state_mlstm_n,
                                          m0_rows, l, segs, False, gain=mlstm_onorm_g[l], hbw=hbw, y=y)
        y = _cmlp(proj_c, cmlp_vnorm_g[l], cmlp_ws, cmlp_bs_t, l, y, (wh + wm) // wc)
        new_h.append(jnp.stack([sh_f, sh_b], axis=1))
        new_c.append(jnp.stack([sc_f, sc_b], axis=1))
        new_n.append(jnp.stack([sn_f, sn_b], axis=1))
        new_m.append(jnp.stack([sm_f[:, 0, 0:hm], sm_b[:, 0, hm:2 * hm]], axis=1))

        x = _mm_residual(y, w_o_b, l, x, mod, 2, bm, _pick(d, (1024, 512, 256)), d, ctx_rows, dec_seq)
        h2 = _normmod(x, norm2_g[l], mod, 3, 4, ctx_rows, dec_seq)
        hid = _mm_swiglu(h2, w_gate_b, w_up_b, l, bm, _pick(d_ffp, (512, 256)))
        x = _mm_residual(hid, w_down_b, l, x, mod, 5, bm, _pick(d, (1024, 512, 256)),
                         _pick(d_ffp, (2816, 2048, 1536, 1024, 768, 512)), ctx_rows, dec_seq)

    y_prompt = _final_norm(x, final_g, 0, ctx_rows).reshape(batch, seq, d)
    y_sample = _final_norm(x, final_g, ctx_rows, dec_batch * dec_seq).reshape(dec_batch, dec_seq, d)
    return (y_prompt, y_sample, jnp.stack(new_h, axis=1), jnp.stack(new_c, axis=1),
            jnp.stack(new_n, axis=1), jnp.stack(new_m, axis=1))
```

```python
import functools

import jax
import jax.numpy as jnp
from jax import lax
from jax.experimental import pallas as pl
from jax.experimental.pallas import tpu as pltpu

F32 = jnp.float32
BF16 = jnp.bfloat16

HEAD_DIM = 128
HGRN_CHUNK = 32
MLSTM_CHUNK = 64
CMLP_CHUNK = 128
EPS = 1e-6
SEG = 256
GATE_LANES = 256
V7X_VMEM_BYTES = 64 * 1024 * 1024
VMEM_CAP_BYTES = V7X_VMEM_BYTES - 6 * 1024 * 1024


def _vmem_limit(block_bytes, temp_bytes=0):
    need = 2 * block_bytes + 2 * temp_bytes + 4 * 1024 * 1024
    return int(min(max(need, 16 * 1024 * 1024), VMEM_CAP_BYTES))


def _pick(n, candidates):
    for c in candidates:
        if c <= n and n % c == 0:
            return c
    return n


def _dot(a, b):
    return jnp.dot(a, b, preferred_element_type=F32)


def _dot_nt(a, b):
    return lax.dot_general(a, b, (((1,), (1,)), ((), ())), preferred_element_type=F32)


def _dot_tn(a, b):
    return lax.dot_general(a, b, (((0,), (0,)), ((), ())), preferred_element_type=F32)


def _sigmoid(x):
    return 0.5 * jnp.tanh(0.5 * x) + 0.5


def _split3(x):
    x1 = x.astype(BF16)
    r1 = x - x1.astype(F32)
    x2 = r1.astype(BF16)
    x3 = (r1 - x2.astype(F32)).astype(BF16)
    return x1, x2, x3


def _chunk_masks(chunk, reverse):
    r = lax.broadcasted_iota(jnp.int32, (SEG, SEG), 0)
    c = lax.broadcasted_iota(jnp.int32, (SEG, SEG), 1)
    same = (r // chunk) == (c // chunk)
    causal = same & ((c >= r) if reverse else (c <= r))
    return same, causal


def _block_expand(x, chunk):
    rows, w = x.shape
    blocks = []
    for c in range(rows // chunk):
        parts = []
        if c > 0:
            parts.append(jnp.zeros((c * chunk, w), x.dtype))
        parts.append(x[c * chunk:(c + 1) * chunk])
        if (c + 1) * chunk < rows:
            parts.append(jnp.zeros((rows - (c + 1) * chunk, w), x.dtype))
        blocks.append(jnp.concatenate(parts, axis=0))
    return jnp.concatenate(blocks, axis=1)


def _chunk_sums(x, same, causal):
    w = x.shape[1]
    one_hot = lambda mask: jnp.where(mask, 1.0, 0.0).astype(BF16)
    lhs = jnp.concatenate([one_hot(causal), one_hot(same)], axis=0)
    pieces = jnp.concatenate(_split3(x), axis=1)
    res = _dot(lhs, pieces)
    res = res[:, :w] + res[:, w:2 * w] + res[:, 2 * w:]
    return res[:SEG], res[SEG:]


def _ada_kernel(c_ref, w_ref, b_ref, o_ref):
    c = c_ref[...]
    a = (c * _sigmoid(c)).astype(BF16)
    o_ref[...] = _dot(a, w_ref[...].astype(BF16)) + b_ref[...]


def _ada_mod(c_all, w_ada, b_ada):
    depth, d, n6 = w_ada.shape
    rows = c_all.shape[0]
    tn = _pick(n6, (512, 256, 128))
    blk = rows * d * 4 + d * tn * 4 + tn * 4 + rows * tn * 4
    return pl.pallas_call(
        _ada_kernel,
        out_shape=jax.ShapeDtypeStruct((depth, rows, n6), F32),
        grid=(depth, n6 // tn),
        in_specs=[
            pl.BlockSpec((rows, d), lambda l, j: (0, 0)),
            pl.BlockSpec((None, d, tn), lambda l, j: (l, 0, j)),
            pl.BlockSpec((None, 1, tn), lambda l, j: (l, 0, j)),
        ],
        out_specs=pl.BlockSpec((None, rows, tn), lambda l, j: (l, 0, j)),
        compiler_params=pltpu.CompilerParams(
            dimension_semantics=("arbitrary", "arbitrary"),
            vmem_limit_bytes=_vmem_limit(blk, d * tn * 2)),
        name="ada_mod",
    )(c_all, w_ada, b_ada.reshape(depth, 1, n6))


def _mod_row(first_row, ctx_rows, lat_len):
    return jnp.where(first_row < ctx_rows, 0, 1 + (first_row - ctx_rows) // lat_len)


def _normmod_kernel(x_ref, g_ref, mod_ref, o_ref, *, shift_idx, scale_idx):
    x = x_ref[...]
    y = x * lax.rsqrt(jnp.mean(x * x, axis=-1, keepdims=True) + EPS) * g_ref[...]
    scale = mod_ref[scale_idx:scale_idx + 1, :]
    shift = mod_ref[shift_idx:shift_idx + 1, :]
    o_ref[...] = (y * (1.0 + scale) + shift).astype(o_ref.dtype)


def _normmod(x, gain, mod, shift_idx, scale_idx, ctx_rows, lat_len):
    n, d = x.shape
    rb = SEG
    blk = rb * d * 4 + d * 4 + 6 * d * 4 + rb * d * 2
    return pl.pallas_call(
        functools.partial(_normmod_kernel, shift_idx=shift_idx, scale_idx=scale_idx),
        out_shape=jax.ShapeDtypeStruct((n, d), BF16),
        grid=(n // rb,),
        in_specs=[
            pl.BlockSpec((rb, d), lambda i: (i, 0)),
            pl.BlockSpec((1, d), lambda i: (0, 0)),
            pl.BlockSpec((None, 6, d), lambda i: (_mod_row(i * rb, ctx_rows, lat_len), 0, 0)),
        ],
        out_specs=pl.BlockSpec((rb, d), lambda i: (i, 0)),
        compiler_params=pltpu.CompilerParams(
            dimension_semantics=("arbitrary",), vmem_limit_bytes=_vmem_limit(blk, 2 * rb * d * 4)),
        name="normmod",
    )(x, gain.reshape(1, d), mod)


def _final_norm_kernel(x_ref, g_ref, o_ref):
    x = x_ref[...]
    o_ref[...] = x * lax.rsqrt(jnp.mean(x * x, axis=-1, keepdims=True) + EPS) * g_ref[...]


def _final_norm(x, gain, row0, rows):
    d = x.shape[1]
    rb = SEG
    blk = 2 * rb * d * 4 + d * 4
    return pl.pallas_call(
        _final_norm_kernel,
        out_shape=jax.ShapeDtypeStruct((rows, d), F32),
        grid=(rows // rb,),
        in_specs=[
            pl.BlockSpec((rb, d), lambda i: (row0 // rb + i, 0)),
            pl.BlockSpec((1, d), lambda i: (0, 0)),
        ],
        out_specs=pl.BlockSpec((rb, d), lambda i: (i, 0)),
        compiler_params=pltpu.CompilerParams(
            dimension_semantics=("arbitrary",), vmem_limit_bytes=_vmem_limit(blk, rb * d * 4)),
        name="final_norm",
    )(x, gain.reshape(1, d))


def _mm_kernel(x_ref, w_ref, o_ref):
    o_ref[...] = _dot(x_ref[...], w_ref[...]).astype(o_ref.dtype)


def _mm_plain(x, w, layer, bm, bn, out_dtype):
    m, k = x.shape
    n = w.shape[2]
    osz = jnp.dtype(out_dtype).itemsize
    blk = bm * k * 2 + k * bn * 2 + bm * bn * osz
    return pl.pallas_call(
        _mm_kernel,
        out_shape=jax.ShapeDtypeStruct((m, n), out_dtype),
        grid=(m // bm, n // bn),
        in_specs=[
            pl.BlockSpec((bm, k), lambda i, j: (i, 0)),
            pl.BlockSpec((None, k, bn), lambda i, j: (layer, 0, j)),
        ],
        out_specs=pl.BlockSpec((bm, bn), lambda i, j: (i, j)),
        compiler_params=pltpu.CompilerParams(
            dimension_semantics=("arbitrary", "arbitrary"),
            vmem_limit_bytes=_vmem_limit(blk, bm * bn * 4)),
        name="proj_matmul",
    )(x, w)


def _swiglu_kernel(x_ref, wg_ref, wu_ref, o_ref, *, n_valid):
    x = x_ref[...]
    g = _dot(x, wg_ref[...])
    u = _dot(x, wu_ref[...])
    val = g * _sigmoid(g) * u
    bn = o_ref.shape[1]
    col = pl.program_id(1) * bn + lax.broadcasted_iota(jnp.int32, val.shape, 1)
    o_ref[...] = jnp.where(col < n_valid, val, 0.0).astype(o_ref.dtype)


def _mm_swiglu(x, wg, wu, layer, bm, bn, n_out):
    m, k = x.shape
    n = n_out
    blk = bm * k * 2 + 2 * k * bn * 2 + bm * bn * 2
    return pl.pallas_call(
        functools.partial(_swiglu_kernel, n_valid=wg.shape[2]),
        out_shape=jax.ShapeDtypeStruct((m, n), BF16),
        grid=(m // bm, n // bn),
        in_specs=[
            pl.BlockSpec((bm, k), lambda i, j: (i, 0)),
            pl.BlockSpec((None, k, bn), lambda i, j: (layer, 0, j)),
            pl.BlockSpec((None, k, bn), lambda i, j: (layer, 0, j)),
        ],
        out_specs=pl.BlockSpec((bm, bn), lambda i, j: (i, j)),
        compiler_params=pltpu.CompilerParams(
            dimension_semantics=("arbitrary", "arbitrary"),
            vmem_limit_bytes=_vmem_limit(blk, 3 * bm * bn * 4)),
        name="ffn_swiglu",
    )(x, wg, wu)


def _mm_residual_kernel(x_ref, w_ref, r_ref, mod_ref, o_ref, acc_ref, *, gate_idx, nk):
    kk = pl.program_id(2)
    p = _dot(x_ref[...], w_ref[...])

    @pl.when(kk == 0)
    def _():
        acc_ref[...] = p

    @pl.when(kk > 0)
    def _():
        acc_ref[...] += p

    @pl.when(kk == nk - 1)
    def _():
        gate = mod_ref[gate_idx:gate_idx + 1, :]
        o_ref[...] = r_ref[...] + gate * acc_ref[...]


def _mm_residual(x, w, layer, res, mod, gate_idx, bm, bn, bk, ctx_rows, lat_len):
    m, k = x.shape
    n = w.shape[2]
    nk = k // bk
    blk = bm * bk * 2 + bk * bn * 2 + 2 * bm * bn * 4 + 6 * bn * 4
    return pl.pallas_call(
        functools.partial(_mm_residual_kernel, gate_idx=gate_idx, nk=nk),
        out_shape=jax.ShapeDtypeStruct((m, n), F32),
        grid=(m // bm, n // bn, nk),
        in_specs=[
            pl.BlockSpec((bm, bk), lambda i, j, kk: (i, kk)),
            pl.BlockSpec((None, bk, bn), lambda i, j, kk: (layer, kk, j)),
            pl.BlockSpec((bm, bn), lambda i, j, kk: (i, j)),
            pl.BlockSpec((None, 6, bn), lambda i, j, kk: (_mod_row(i * bm, ctx_rows, lat_len), 0, j)),
        ],
        out_specs=pl.BlockSpec((bm, bn), lambda i, j, kk: (i, j)),
        scratch_shapes=[pltpu.VMEM((bm, bn), F32)],
        compiler_params=pltpu.CompilerParams(
            dimension_semantics=("arbitrary", "arbitrary", "arbitrary"),
            vmem_limit_bytes=_vmem_limit(blk, 2 * bm * bn * 4)),
        name="residual_matmul",
    )(x, w, res, mod)


class _Segs:
    def __init__(self, ctx_seqs, ctx_len, lat_seqs, lat_len):
        self.ctx_sps = ctx_len // SEG
        self.lat_sps = lat_len // SEG
        self.ctx_seqs = ctx_seqs
        self.lat_seqs = lat_seqs
        self.ctx_segs = ctx_seqs * self.ctx_sps
        self.nseg = self.ctx_segs + lat_seqs * self.lat_sps

    def seg_of_step(self, i, reverse):
        return (self.nseg - 1 - i) if reverse else i

    def position(self, seg):
        is_ctx = seg < self.ctx_segs
        pos = jnp.where(is_ctx, seg % self.ctx_sps, (seg - self.ctx_segs) % self.lat_sps)
        sps = jnp.where(is_ctx, self.ctx_sps, self.lat_sps)
        return is_ctx, pos, sps

    def lat_seq(self, seg):
        return jnp.clip((seg - self.ctx_segs) // self.lat_sps, 0, self.lat_seqs - 1)

    def ctx_seq(self, seg):
        return jnp.clip(seg // self.ctx_sps, 0, self.ctx_seqs - 1)


def _seq_edges(segs, seg, reverse):
    is_ctx, pos, sps = segs.position(seg)
    first = pos == (sps - 1 if reverse else 0)
    last = pos == (0 if reverse else sps - 1)
    return is_ctx, first, last


def _hgrn_kernel(*refs, segs, heads, reverse):
    if reverse:
        q_ref, v_ref, f_ref, lb_ref, s0_ref, o_ref, sfin_ref, st_ref = refs
    else:
        (q_ref, v_ref, f_ref, lb_ref, s0_ref, gate_ref, obw_ref, gain_ref,
         o_ref, sfin_ref, st_ref) = refs
    seg = segs.seg_of_step(pl.program_id(0), reverse)
    is_ctx, first, last = _seq_edges(segs, seg, reverse)

    @pl.when(first & is_ctx)
    def _():
        st_ref[...] = jnp.zeros_like(st_ref)

    @pl.when(first & jnp.logical_not(is_ctx))
    def _():
        for h in range(heads):
            st_ref[h] = s0_ref[h].T

    _, causal = _chunk_masks(HGRN_CHUNK, reverse)
    tri = jnp.where(causal, 1.0, 0.0).astype(BF16)
    n_chunks = SEG // HGRN_CHUNK
    order = range(n_chunks - 1, -1, -1) if reverse else range(n_chunks)
    end_row = 0 if reverse else HGRN_CHUNK - 1
    hs = range(heads)
    sls = [slice(h * HEAD_DIM, (h + 1) * HEAD_DIM) for h in hs]

    ks, cums = [], []
    for h in hs:
        lb = lb_ref[:, sls[h]]
        f = lb + (1.0 - lb) * _sigmoid(f_ref[:, sls[h]])
        ks.append(1.0 - f)
        cums.append(_dot(tri, jnp.concatenate(_split3(jnp.log(f)), axis=1)))

    qd, ki, kex, vb, decay = [], [], [], [], []
    for h in hs:
        r = cums[h]
        b = r[:, :HEAD_DIM] + r[:, HEAD_DIM:2 * HEAD_DIM] + r[:, 2 * HEAD_DIM:]
        ends = [b[c * HGRN_CHUNK + end_row:c * HGRN_CHUNK + end_row + 1] for c in range(n_chunks)]
        bend = jnp.concatenate([jnp.broadcast_to(e, (HGRN_CHUNK, HEAD_DIM)) for e in ends], axis=0)
        qd.append((q_ref[:, sls[h]] * jnp.exp(b)).astype(BF16))
        ki.append((ks[h] * jnp.exp(-b)).astype(BF16))
        kex.append(_block_expand((ks[h] * jnp.exp(bend - b)).astype(BF16), HGRN_CHUNK))
        vb.append(v_ref[:, sls[h]].astype(BF16))
        decay.append([jnp.exp(e) for e in ends])

    att = [_dot_nt(qd[h], ki[h]) for h in hs]

    o_intra, kvt = [], []
    for h in hs:
        o_intra.append(_dot(jnp.where(causal, att[h], 0.0).astype(BF16), vb[h]))
        kvt.append(_dot_tn(vb[h], kex[h]))

    st_in = [[None] * n_chunks for _ in hs]
    for h in hs:
        st = st_ref[h]
        for c in order:
            st_in[h][c] = st.astype(BF16)
            st = st * decay[h][c] + kvt[h][:, c * HEAD_DIM:(c + 1) * HEAD_DIM]
        st_ref[h] = st

    o_state = [[_dot_nt(qd[h][c * HGRN_CHUNK:(c + 1) * HGRN_CHUNK], st_in[h][c]) for c in range(n_chunks)]
               for h in hs]

    for h in hs:
        o = o_intra[h] + jnp.concatenate(o_state[h], axis=0)
        if reverse:
            o_ref[:, sls[h]] = o
        else:
            tot = o + obw_ref[:, sls[h]]
            y = tot * lax.rsqrt(jnp.mean(tot * tot, axis=-1, keepdims=True) + EPS) * gain_ref[:, sls[h]]
            g = gate_ref[:, sls[h]]
            o_ref[:, sls[h]] = (y * (g * _sigmoid(g))).astype(o_ref.dtype)

    @pl.when(last & is_ctx)
    def _():
        for h in range(heads):
            sfin_ref[h] = st_ref[h].T


def _hgrn_pass(proj, lb, state, layer, segs, reverse, *, gain=None, obw=None, y_cols=None):
    n = proj.shape[0]
    w = proj.shape[1] // 5
    heads = w // HEAD_DIM
    d = 1 if reverse else 0

    def seg_idx(i):
        return segs.seg_of_step(i, reverse)

    col = lambda cb: pl.BlockSpec((SEG, w), lambda i: (seg_idx(i), cb))
    in_specs = [
        col(0), col(1), col(4 if reverse else 3),
        pl.BlockSpec((None, 1, w), lambda i: (d, 0, 0)),
        pl.BlockSpec((None, None, None, heads, HEAD_DIM, HEAD_DIM),
                     lambda i: (segs.lat_seq(seg_idx(i)), layer, d, 0, 0, 0)),
    ]
    args = [proj, proj, proj, lb, state]
    sfin_shape = jax.ShapeDtypeStruct((segs.ctx_seqs, heads, HEAD_DIM, HEAD_DIM), F32)
    sfin_spec = pl.BlockSpec((None, heads, HEAD_DIM, HEAD_DIM),
                             lambda i: (segs.ctx_seq(seg_idx(i)), 0, 0, 0))
    blk = 3 * SEG * w * 4 + w * 4 + 2 * heads * HEAD_DIM * HEAD_DIM * 4
    if reverse:
        out_shape = [jax.ShapeDtypeStruct((n, w), F32), sfin_shape]
        out_specs = [pl.BlockSpec((SEG, w), lambda i: (seg_idx(i), 0)), sfin_spec]
        blk += SEG * w * 4
    else:
        in_specs += [col(2), pl.BlockSpec((SEG, w), lambda i: (i, 0)),
                     pl.BlockSpec((1, w), lambda i: (0, 0))]
        args += [proj, obw, gain.reshape(1, w)]
        out_shape = [jax.ShapeDtypeStruct((n, y_cols), BF16), sfin_shape]
        out_specs = [pl.BlockSpec((SEG, w), lambda i: (i, 0)), sfin_spec]
        blk += 2 * SEG * w * 4 + SEG * w * 2 + w * 4
    return pl.pallas_call(
        functools.partial(_hgrn_kernel, segs=segs, heads=heads, reverse=reverse),
        out_shape=out_shape,
        grid=(segs.nseg,),
        in_specs=in_specs,
        out_specs=out_specs,
        scratch_shapes=[pltpu.VMEM((heads, HEAD_DIM, HEAD_DIM), F32)],
        compiler_params=pltpu.CompilerParams(
            dimension_semantics=("arbitrary",),
            vmem_limit_bytes=_vmem_limit(blk, 16 * 1024 * 1024)),
        name="hgrn_bwd" if reverse else "hgrn_fwd",
    )(*args)


def _running_max(x, chunk, reverse):
    rows = x.shape[0]
    pos = lax.broadcasted_iota(jnp.int32, x.shape, 0) % chunk
    step = 1
    while step < chunk:
        if reverse:
            shifted = pltpu.roll(x, rows - step, axis=0)
            ok = pos < chunk - step
        else:
            shifted = pltpu.roll(x, step, axis=0)
            ok = pos >= step
        x = jnp.maximum(x, jnp.where(ok, shifted, -jnp.inf))
        step *= 2
    return x


def _mlstm_kernel(*refs, segs, heads, reverse):
    if reverse:
        (q_ref, k_ref, v_ref, gt_ref, gb_ref, c0_ref, n0_ref, m0_ref,
         o_ref, cfin_ref, nfin_ref, mfin_ref, ct_ref, n_ref, m_ref) = refs
    else:
        (q_ref, k_ref, v_ref, gt_ref, gb_ref, c0_ref, n0_ref, m0_ref, og_ref, hbw_ref, gain_ref, _y_in,
         o_ref, cfin_ref, nfin_ref, mfin_ref, ct_ref, n_ref, m_ref) = refs
    seg = segs.seg_of_step(pl.program_id(0), reverse)
    is_ctx, first, last = _seq_edges(segs, seg, reverse)
    lane0 = heads if reverse else 0

    @pl.when(first & is_ctx)
    def _():
        ct_ref[...] = jnp.zeros_like(ct_ref)
        n_ref[...] = jnp.zeros_like(n_ref)
        m_ref[...] = jnp.zeros_like(m_ref)

    @pl.when(first & jnp.logical_not(is_ctx))
    def _():
        for h in range(heads):
            ct_ref[h] = c0_ref[h].T
        n_ref[...] = n0_ref[...]
        m_ref[...] = m0_ref[...]

    half = GATE_LANES // 2
    gates = gt_ref[...] + gb_ref[...]
    ig = gates[:, :half]
    fpre = gates[:, half:]
    lf = jnp.minimum(fpre, 0.0) - jnp.log1p(jnp.exp(-jnp.abs(fpre)))
    same, causal = _chunk_masks(MLSTM_CHUNK, reverse)
    b, _ = _chunk_sums(lf, same, causal)
    u = ig - b
    mrun = _running_max(u, MLSTM_CHUNK, reverse)
    n_chunks = SEG // MLSTM_CHUNK
    order = list(range(n_chunks - 1, -1, -1) if reverse else range(n_chunks))
    m_cur = m_ref[...]
    mu_parts, mul_parts, mcur_parts, dec_rows = ([None] * n_chunks for _ in range(4))
    for c in order:
        r0 = c * MLSTM_CHUNK
        end = r0 if reverse else r0 + MLSTM_CHUNK - 1
        mu_c = jnp.maximum(m_cur, mrun[r0:r0 + MLSTM_CHUNK])
        mu_end = mu_c[end - r0:end - r0 + 1]
        mu_parts[c] = mu_c
        mul_parts[c] = jnp.broadcast_to(mu_end, mu_c.shape)
        mcur_parts[c] = jnp.broadcast_to(m_cur, mu_c.shape)
        dec_rows[c] = jnp.exp(m_cur - mu_end)
        m_cur = b[end:end + 1] + mu_end
    m_ref[...] = m_cur
    mu = jnp.concatenate(mu_parts, axis=0)
    aw_all = jnp.exp(jnp.concatenate(mcur_parts, axis=0) - mu)
    emt_all = jnp.exp(-(b + mu))
    w_all = jnp.exp(u - jnp.concatenate(mul_parts, axis=0))
    u_t = u.T

    scale = HEAD_DIM ** -0.5
    hs = range(heads)
    sls = [slice(h * HEAD_DIM, (h + 1) * HEAD_DIM) for h in hs]
    chunk_rows = [slice(c * MLSTM_CHUNK, (c + 1) * MLSTM_CHUNK) for c in range(n_chunks)]

    qs = [(q_ref[:, sls[h]] * scale).astype(BF16) for h in hs]
    kb = [k_ref[:, sls[h]].astype(BF16) for h in hs]
    vb = [v_ref[:, sls[h]].astype(BF16) for h in hs]
    scores = [_dot_nt(qs[h], kb[h]) for h in hs]

    rowsum, num_intra, kvt, nsum = [], [], [], []
    for h in hs:
        j = lane0 + h
        dmat = jnp.where(causal, jnp.exp(u_t[j:j + 1, :] - mu[:, j:j + 1]), 0.0)
        sc = scores[h] * dmat
        rowsum.append(jnp.sum(sc, axis=1, keepdims=True))
        num_intra.append(_dot(sc.astype(BF16), vb[h]))
        wb = jnp.broadcast_to(w_all[:, j:j + 1], (SEG, HEAD_DIM))
        wk = (wb * k_ref[:, sls[h]]).astype(BF16)
        kvt.append(_dot_tn(vb[h], _block_expand(wk, MLSTM_CHUNK)))
        wn = wb.astype(BF16).astype(F32) * kb[h].astype(F32)
        nsum.append([jnp.sum(wn[rs], axis=0, keepdims=True) for rs in chunk_rows])

    rhs = [[None] * n_chunks for _ in hs]
    for h in hs:
        j = lane0 + h
        ct = ct_ref[h]
        nv = n_ref[h:h + 1, :]
        for c in order:
            rhs[h][c] = jnp.concatenate([ct, nv, jnp.zeros((15, HEAD_DIM), F32)], axis=0).astype(BF16)
            dec = dec_rows[c][:, j:j + 1]
            ct = dec * ct + kvt[h][:, c * HEAD_DIM:(c + 1) * HEAD_DIM]
            nv = dec * nv + nsum[h][c]
        ct_ref[h] = ct
        n_ref[h:h + 1, :] = nv

    qc = [[_dot_nt(qs[h][chunk_rows[c]], rhs[h][c]) for c in range(n_chunks)] for h in hs]

    for h in hs:
        j = lane0 + h
        qcat = jnp.concatenate(qc[h], axis=0)
        aw = aw_all[:, j:j + 1]
        num = aw * qcat[:, :HEAD_DIM] + num_intra[h]
        den = aw * qcat[:, HEAD_DIM:HEAD_DIM + 1] + rowsum[h]
        hv = num / jnp.maximum(jnp.abs(den), emt_all[:, j:j + 1])
        if reverse:
            o_ref[:, sls[h]] = hv
        else:
            tot = hv + hbw_ref[:, sls[h]]
            y = tot * lax.rsqrt(jnp.mean(tot * tot, axis=-1, keepdims=True) + EPS) * gain_ref[:, sls[h]]
            o_ref[:, sls[h]] = (_sigmoid(og_ref[:, sls[h]]) * y).astype(o_ref.dtype)

    @pl.when(last & is_ctx)
    def _():
        for h in range(heads):
            cfin_ref[h] = ct_ref[h].T
        nfin_ref[...] = n_ref[...]
        mfin_ref[...] = m_ref[...]


def _mlstm_pass(proj, gate_b, c0, n0, m0, layer, segs, reverse, *, gain=None, hbw=None, y=None):
    n = proj.shape[0]
    w = (proj.shape[1] - GATE_LANES) // 4
    heads = w // HEAD_DIM
    d = 1 if reverse else 0

    def seg_idx(i):
        return segs.seg_of_step(i, reverse)

    def lat(i):
        return segs.lat_seq(seg_idx(i))

    col = lambda cb: pl.BlockSpec((SEG, w), lambda i: (seg_idx(i), cb))
    in_specs = [
        col(0), col(1), col(2),
        pl.BlockSpec((SEG, GATE_LANES), lambda i: (seg_idx(i), 4 * w // GATE_LANES)),
        pl.BlockSpec((1, GATE_LANES), lambda i: (0, 0)),
        pl.BlockSpec((None, None, None, heads, HEAD_DIM, HEAD_DIM), lambda i: (lat(i), layer, d, 0, 0, 0)),
        pl.BlockSpec((None, None, None, heads, HEAD_DIM), lambda i: (lat(i), layer, d, 0, 0)),
        pl.BlockSpec((None, None, 1, HEAD_DIM), lambda i: (lat(i), layer, 0, 0)),
    ]
    args = [proj, proj, proj, proj, gate_b, c0, n0, m0]
    fin_shapes = [
        jax.ShapeDtypeStruct((segs.ctx_seqs, heads, HEAD_DIM, HEAD_DIM), F32),
        jax.ShapeDtypeStruct((segs.ctx_seqs, heads, HEAD_DIM), F32),
        jax.ShapeDtypeStruct((segs.ctx_seqs, 1, HEAD_DIM), F32),
    ]

    def ctx(i):
        return segs.ctx_seq(seg_idx(i))

    fin_specs = [
        pl.BlockSpec((None, heads, HEAD_DIM, HEAD_DIM), lambda i: (ctx(i), 0, 0, 0)),
        pl.BlockSpec((None, heads, HEAD_DIM), lambda i: (ctx(i), 0, 0)),
        pl.BlockSpec((None, 1, HEAD_DIM), lambda i: (ctx(i), 0, 0)),
    ]
    blk = 3 * SEG * w * 4 + SEG * GATE_LANES * 4 + 2 * heads * HEAD_DIM * HEAD_DIM * 4
    if reverse:
        out_shape = [jax.ShapeDtypeStruct((n, w), F32)] + fin_shapes
        out_specs = [pl.BlockSpec((SEG, w), lambda i: (seg_idx(i), 0))] + fin_specs
        aliases = {}
        blk += SEG * w * 4
    else:
        in_specs += [col(3), pl.BlockSpec((SEG, w), lambda i: (i, 0)),
                     pl.BlockSpec((1, w), lambda i: (0, 0)),
                     pl.BlockSpec(memory_space=pl.ANY)]
        args += [proj, hbw, gain.reshape(1, w), y]
        out_shape = [jax.ShapeDtypeStruct(y.shape, y.dtype)] + fin_shapes
        out_specs = [pl.BlockSpec((SEG, w), lambda i: (i, 1))] + fin_specs
        aliases = {11: 0}
        blk += 2 * SEG * w * 4 + SEG * w * 2
    return pl.pallas_call(
        functools.partial(_mlstm_kernel, segs=segs, heads=heads, reverse=reverse),
        out_shape=out_shape,
        grid=(segs.nseg,),
        in_specs=in_specs,
        out_specs=out_specs,
        scratch_shapes=[pltpu.VMEM((heads, HEAD_DIM, HEAD_DIM), F32),
                        pltpu.VMEM((heads, HEAD_DIM), F32),
                        pltpu.VMEM((1, HEAD_DIM), F32)],
        input_output_aliases=aliases,
        compiler_params=pltpu.CompilerParams(
            dimension_semantics=("arbitrary",),
            vmem_limit_bytes=_vmem_limit(blk, 16 * 1024 * 1024)),
        name="mlstm_bwd" if reverse else "mlstm_fwd",
    )(*args)


def _gelu(x):
    return 0.5 * x * (1.0 + jnp.tanh(0.7978845608028654 * (x + 0.044715 * (x * x * x))))


def _cmlp_kernel(u_ref, v_ref, g_ref, w_ref, b_ref, _y_in, o_ref, *, groups):
    gv = _gelu(v_ref[...])
    vv = (gv * lax.rsqrt(jnp.mean(gv * gv, axis=-1, keepdims=True) + EPS) * g_ref[...]).astype(BF16)
    for g in range(groups):
        sl = slice(g * HEAD_DIM, (g + 1) * HEAD_DIM)
        wg = w_ref[g].astype(BF16)
        bias = b_ref[:, g:g + 1]
        for c in range(SEG // CMLP_CHUNK):
            rs = slice(c * CMLP_CHUNK, (c + 1) * CMLP_CHUNK)
            mixed = _dot(wg, vv[rs, sl]) + bias
            o_ref[rs, sl] = (_gelu(u_ref[rs, sl]) * mixed).astype(o_ref.dtype)


def _cmlp(proj, gain, ws, bs_t, layer, y, col_block):
    n = proj.shape[0]
    w = proj.shape[1] // 2
    groups = w // HEAD_DIM
    blk = 2 * SEG * w * 4 + w * 4 + groups * CMLP_CHUNK * CMLP_CHUNK * 4 + CMLP_CHUNK * 128 * 4 + SEG * w * 2
    return pl.pallas_call(
        functools.partial(_cmlp_kernel, groups=groups),
        out_shape=jax.ShapeDtypeStruct(y.shape, y.dtype),
        grid=(n // SEG,),
        in_specs=[
            pl.BlockSpec((SEG, w), lambda i: (i, 0)),
            pl.BlockSpec((SEG, w), lambda i: (i, 1)),
            pl.BlockSpec((1, w), lambda i: (0, 0)),
            pl.BlockSpec((None, groups, CMLP_CHUNK, CMLP_CHUNK), lambda i: (layer, 0, 0, 0)),
            pl.BlockSpec((None, CMLP_CHUNK, groups), lambda i: (layer, 0, 0)),
            pl.BlockSpec(memory_space=pl.ANY),
        ],
        out_specs=pl.BlockSpec((SEG, w), lambda i: (i, col_block)),
        input_output_aliases={5: 0},
        compiler_params=pltpu.CompilerParams(
            dimension_semantics=("arbitrary",), vmem_limit_bytes=_vmem_limit(blk, 8 * 1024 * 1024)),
        name="cmlp",
    )(proj, proj, gain.reshape(1, w), ws, bs_t, y)


def kernel(x_prompt, x_sample, state_hgrn, state_mlstm_c, state_mlstm_n, state_mlstm_m, c, c_ctx, w_in, mlstm_gate_b, hgrn_lb_logits, hgrn_onorm_g, mlstm_onorm_g, cmlp_vnorm_g, cmlp_ws, cmlp_bs, w_o, norm1_g, norm2_g, w_ada, b_ada, w_ffn_gate, w_ffn_up, w_ffn_down, final_g):
    batch, seq, d = x_prompt.shape
    dec_batch, dec_seq, _ = x_sample.shape
    depth = w_in.shape[0]
    wh = hgrn_onorm_g.shape[1]
    wm = mlstm_onorm_g.shape[1]
    wc = cmlp_vnorm_g.shape[1]
    hh, hm = wh // HEAD_DIM, wm // HEAD_DIM
    d_ff = w_ffn_gate.shape[2]
    assert seq % SEG == 0 and dec_seq % SEG == 0 and wh == wm
    assert 2 * hm <= GATE_LANES // 2 and (wh + wm) % wc == 0 and wh + wm + wc == d
    ctx_rows = batch * seq
    n = ctx_rows + dec_batch * dec_seq
    segs = _Segs(batch, seq, dec_batch, dec_seq)

    o_m = 5 * wh
    o_g = o_m + 4 * wm
    o_c = o_g + 4 * hm
    w_h = w_in[:, :, :o_m].astype(BF16)
    wg_cols = w_in[:, :, o_g:o_c]
    zpad = jnp.zeros((depth, d, GATE_LANES // 2 - 2 * hm), w_in.dtype)
    w_m = jnp.concatenate(
        [w_in[:, :, o_m:o_g], wg_cols[:, :, 0:hm], wg_cols[:, :, 2 * hm:3 * hm], zpad,
         wg_cols[:, :, hm:2 * hm], wg_cols[:, :, 3 * hm:4 * hm], zpad], axis=2).astype(BF16)
    w_c = w_in[:, :, o_c:].astype(BF16)
    bpad = jnp.zeros((depth, GATE_LANES // 2 - 2 * hm), F32)
    gb = mlstm_gate_b.astype(F32)
    gate_b = jnp.concatenate(
        [gb[:, 0:hm], gb[:, 2 * hm:3 * hm], bpad, gb[:, hm:2 * hm], gb[:, 3 * hm:4 * hm], bpad], axis=1)
    w_o_b = w_o.astype(BF16)
    ff_pad = (-d_ff) % 1024
    w_gate_b = w_ffn_gate.astype(BF16)
    w_up_b = w_ffn_up.astype(BF16)
    w_down_b = jnp.pad(w_ffn_down.astype(BF16), ((0, 0), (0, ff_pad), (0, 0)))
    d_ffp = d_ff + ff_pad

    lb_all = jnp.cumsum(jax.nn.softmax(hgrn_lb_logits.astype(F32), axis=0), axis=0)
    m0_rows = jnp.concatenate(
        [state_mlstm_m[:, :, 0, :], state_mlstm_m[:, :, 1, :],
         jnp.zeros((dec_batch, depth, HEAD_DIM - 2 * hm), F32)], axis=-1)[:, :, None, :]
    cmlp_bs_t = jnp.swapaxes(cmlp_bs, 1, 2)

    ada_rows = 16
    c_all = jnp.concatenate(
        [c_ctx[None, :], c, jnp.zeros((ada_rows - 1 - dec_batch, d), F32)], axis=0)
    mod_all = _ada_mod(c_all, w_ada, b_ada)[:, :1 + dec_batch].reshape(depth, 1 + dec_batch, 6, d)

    bm = _pick(min(ctx_rows, dec_seq), (1024, 512, 256))
    x = jnp.concatenate([x_prompt.reshape(ctx_rows, d), x_sample.reshape(dec_batch * dec_seq, d)], axis=0)
    new_h, new_c, new_n, new_m = [], [], [], []
    for l in range(depth):
        mod = mod_all[l]
        h = _normmod(x, norm1_g[l], mod, 0, 1, ctx_rows, dec_seq)
        proj_h = _mm_plain(h, w_h, l, bm, _pick(5 * wh, (768, 640, 512, 384, 256, 128)), F32)
        proj_m = _mm_plain(h, w_m, l, bm, _pick(4 * wm + GATE_LANES, (1280, 1024, 896, 640, 512, 256)), F32)
        proj_c = _mm_plain(h, w_c, l, bm, _pick(2 * wc, (1024, 512, 256)), F32)

        obw, sh_b = _hgrn_pass(proj_h, lb_all[l][:, None, :], state_hgrn, l, segs, True)
        y, sh_f = _hgrn_pass(proj_h, lb_all[l][:, None, :], state_hgrn, l, segs, False,
                             gain=hgrn_onorm_g[l], obw=obw, y_cols=d)
        hbw, sc_b, sn_b, sm_b = _mlstm_pass(proj_m, gate_b[l][None, :], state_mlstm_c, state_mlstm_n,
                                            m0_rows, l, segs, True)
        y, sc_f, sn_f, sm_f = _mlstm_pass(proj_m, gate_b[l][None, :], state_mlstm_c, state_mlstm_n,
                                          m0_rows, l, segs, False, gain=mlstm_onorm_g[l], hbw=hbw, y=y)
        y = _cmlp(proj_c, cmlp_vnorm_g[l], cmlp_ws, cmlp_bs_t, l, y, (wh + wm) // wc)
        new_h.append(jnp.stack([sh_f, sh_b], axis=1))
        new_c.append(jnp.stack([sc_f, sc_b], axis=1))
        new_n.append(jnp.stack([sn_f, sn_b], axis=1))
        new_m.append(jnp.stack([sm_f[:, 0, 0:hm], sm_b[:, 0, hm:2 * hm]], axis=1))

        x = _mm_residual(y, w_o_b, l, x, mod, 2, bm, _pick(d, (1024, 512, 256)), d, ctx_rows, dec_seq)
        h2 = _normmod(x, norm2_g[l], mod, 3, 4, ctx_rows, dec_seq)
        hid = _mm_swiglu(h2, w_gate_b, w_up_b, l, bm, _pick(d_ffp, (512, 256)), d_ffp)
        x = _mm_residual(hid, w_down_b, l, x, mod, 5, bm, _pick(d, (1024, 512, 256)),
                         _pick(d_ffp, (2816, 2048, 1536, 1024, 768, 512)), ctx_rows, dec_seq)

    y_prompt = _final_norm(x, final_g, 0, ctx_rows).reshape(batch, seq, d)
    y_sample = _final_norm(x, final_g, ctx_rows, dec_batch * dec_seq).reshape(dec_batch, dec_seq, d)
    return (y_prompt, y_sample, jnp.stack(new_h, axis=1), jnp.stack(new_c, axis=1),
            jnp.stack(new_n, axis=1), jnp.stack(new_m, axis=1))
```

```python
import functools

import jax
import jax.numpy as jnp
from jax import lax
from jax.experimental import pallas as pl
from jax.experimental.pallas import tpu as pltpu

F32 = jnp.float32
BF16 = jnp.bfloat16

HEAD_DIM = 128
HGRN_CHUNK = 32
MLSTM_CHUNK = 64
CMLP_CHUNK = 128
EPS = 1e-6
SEG = 256
GATE_LANES = 256
HGRN_GROUP, HGRN_SKEW = 3, 2
MLSTM_GROUP, MLSTM_SKEW = 6, 8
V7X_VMEM_BYTES = 64 * 1024 * 1024
VMEM_CAP_BYTES = V7X_VMEM_BYTES - 6 * 1024 * 1024


def _vmem_limit(block_bytes, temp_bytes=0):
    need = 2 * block_bytes + 2 * temp_bytes + 4 * 1024 * 1024
    return int(min(max(need, 16 * 1024 * 1024), VMEM_CAP_BYTES))


def _pick(n, candidates):
    for c in candidates:
        if c <= n and n % c == 0:
            return c
    return n


def _dot(a, b):
    return jnp.dot(a, b, preferred_element_type=F32)


def _dot_nt(a, b):
    return lax.dot_general(a, b, (((1,), (1,)), ((), ())), preferred_element_type=F32)


def _dot_tn(a, b):
    return lax.dot_general(a, b, (((0,), (0,)), ((), ())), preferred_element_type=F32)


def _sigmoid(x):
    return 0.5 * jnp.tanh(0.5 * x) + 0.5


def _split3(x):
    x1 = x.astype(BF16)
    r1 = x - x1.astype(F32)
    x2 = r1.astype(BF16)
    x3 = (r1 - x2.astype(F32)).astype(BF16)
    return x1, x2, x3


def _split2(x):
    x1 = x.astype(BF16)
    return x1, (x - x1.astype(F32)).astype(BF16)


def _chunk_masks(chunk, reverse):
    r = lax.broadcasted_iota(jnp.int32, (SEG, SEG), 0)
    c = lax.broadcasted_iota(jnp.int32, (SEG, SEG), 1)
    same = (r // chunk) == (c // chunk)
    causal = same & ((c >= r) if reverse else (c <= r))
    return same, causal


def _run_skewed(stage_gens, skew):
    live = list(enumerate(stage_gens))
    tick = 0
    while live:
        for item in list(live):
            g, gen = item
            if tick >= g * skew:
                try:
                    next(gen)
                except StopIteration:
                    live.remove(item)
        tick += 1


def _block_expand(x, chunk):
    rows, w = x.shape
    blocks = []
    for c in range(rows // chunk):
        parts = []
        if c > 0:
            parts.append(jnp.zeros((c * chunk, w), x.dtype))
        parts.append(x[c * chunk:(c + 1) * chunk])
        if (c + 1) * chunk < rows:
            parts.append(jnp.zeros((rows - (c + 1) * chunk, w), x.dtype))
        blocks.append(jnp.concatenate(parts, axis=0))
    return jnp.concatenate(blocks, axis=1)


def _ada_kernel(c_ref, w_ref, b_ref, o_ref):
    c = c_ref[...]
    a = (c * _sigmoid(c)).astype(BF16)
    o_ref[...] = _dot(a, w_ref[...].astype(BF16)) + b_ref[...]


def _ada_mod(c_all, w_ada, b_ada):
    depth, d, n6 = w_ada.shape
    rows = c_all.shape[0]
    tn = _pick(n6, (512, 256, 128))
    blk = rows * d * 4 + d * tn * 4 + tn * 4 + rows * tn * 4
    return pl.pallas_call(
        _ada_kernel,
        out_shape=jax.ShapeDtypeStruct((depth, rows, n6), F32),
        grid=(depth, n6 // tn),
        in_specs=[
            pl.BlockSpec((rows, d), lambda l, j: (0, 0)),
            pl.BlockSpec((None, d, tn), lambda l, j: (l, 0, j)),
            pl.BlockSpec((None, 1, tn), lambda l, j: (l, 0, j)),
        ],
        out_specs=pl.BlockSpec((None, rows, tn), lambda l, j: (l, 0, j)),
        compiler_params=pltpu.CompilerParams(
            dimension_semantics=("arbitrary", "arbitrary"),
            vmem_limit_bytes=_vmem_limit(blk, d * tn * 2)),
        name="ada_mod",
    )(c_all, w_ada, b_ada.reshape(depth, 1, n6))


def _mod_row(first_row, ctx_rows, lat_len):
    return jnp.where(first_row < ctx_rows, 0, 1 + (first_row - ctx_rows) // lat_len)


def _normmod_kernel(x_ref, g_ref, mod_ref, o_ref, *, shift_idx, scale_idx):
    x = x_ref[...]
    y = x * lax.rsqrt(jnp.mean(x * x, axis=-1, keepdims=True) + EPS) * g_ref[...]
    scale = mod_ref[scale_idx:scale_idx + 1, :]
    shift = mod_ref[shift_idx:shift_idx + 1, :]
    o_ref[...] = (y * (1.0 + scale) + shift).astype(o_ref.dtype)


def _normmod(x, gain, mod, shift_idx, scale_idx, ctx_rows, lat_len):
    n, d = x.shape
    rb = 2 * SEG if ctx_rows % (2 * SEG) == 0 and lat_len % (2 * SEG) == 0 else SEG
    blk = rb * d * 4 + d * 4 + 6 * d * 4 + rb * d * 2
    return pl.pallas_call(
        functools.partial(_normmod_kernel, shift_idx=shift_idx, scale_idx=scale_idx),
        out_shape=jax.ShapeDtypeStruct((n, d), BF16),
        grid=(n // rb,),
        in_specs=[
            pl.BlockSpec((rb, d), lambda i: (i, 0)),
            pl.BlockSpec((1, d), lambda i: (0, 0)),
            pl.BlockSpec((None, 6, d), lambda i: (_mod_row(i * rb, ctx_rows, lat_len), 0, 0)),
        ],
        out_specs=pl.BlockSpec((rb, d), lambda i: (i, 0)),
        compiler_params=pltpu.CompilerParams(
            dimension_semantics=("arbitrary",), vmem_limit_bytes=_vmem_limit(blk, 2 * rb * d * 4)),
        name="normmod",
    )(x, gain.reshape(1, d), mod)


def _final_norm_kernel(x_ref, g_ref, o_ref):
    x = x_ref[...]
    o_ref[...] = x * lax.rsqrt(jnp.mean(x * x, axis=-1, keepdims=True) + EPS) * g_ref[...]


def _final_norm(x, gain, row0, rows):
    d = x.shape[1]
    rb = 2 * SEG if row0 % (2 * SEG) == 0 and rows % (2 * SEG) == 0 else SEG
    blk = 2 * rb * d * 4 + d * 4
    return pl.pallas_call(
        _final_norm_kernel,
        out_shape=jax.ShapeDtypeStruct((rows, d), F32),
        grid=(rows // rb,),
        in_specs=[
            pl.BlockSpec((rb, d), lambda i: (row0 // rb + i, 0)),
            pl.BlockSpec((1, d), lambda i: (0, 0)),
        ],
        out_specs=pl.BlockSpec((rb, d), lambda i: (i, 0)),
        compiler_params=pltpu.CompilerParams(
            dimension_semantics=("arbitrary",), vmem_limit_bytes=_vmem_limit(blk, rb * d * 4)),
        name="final_norm",
    )(x, gain.reshape(1, d))


def _mm_kernel(x_ref, w_ref, o_ref):
    o_ref[...] = _dot(x_ref[...], w_ref[...]).astype(o_ref.dtype)


def _mm_plain(x, w, layer, bm, bn, out_dtype):
    m, k = x.shape
    n = w.shape[2]
    osz = jnp.dtype(out_dtype).itemsize
    blk = bm * k * 2 + k * bn * 2 + bm * bn * osz
    return pl.pallas_call(
        _mm_kernel,
        out_shape=jax.ShapeDtypeStruct((m, n), out_dtype),
        grid=(m // bm, n // bn),
        in_specs=[
            pl.BlockSpec((bm, k), lambda i, j: (i, 0)),
            pl.BlockSpec((None, k, bn), lambda i, j: (layer, 0, j)),
        ],
        out_specs=pl.BlockSpec((bm, bn), lambda i, j: (i, j)),
        compiler_params=pltpu.CompilerParams(
            dimension_semantics=("arbitrary", "arbitrary"),
            vmem_limit_bytes=_vmem_limit(blk, bm * bn * 4)),
        name="proj_matmul",
    )(x, w)


def _swiglu_kernel(x_ref, wg_ref, wu_ref, o_ref, *, n_valid):
    x = x_ref[...]
    g = _dot(x, wg_ref[...])
    u = _dot(x, wu_ref[...])
    val = g * _sigmoid(g) * u
    bn = o_ref.shape[1]
    col = pl.program_id(1) * bn + lax.broadcasted_iota(jnp.int32, val.shape, 1)
    o_ref[...] = jnp.where(col < n_valid, val, 0.0).astype(o_ref.dtype)


def _mm_swiglu(x, wg, wu, layer, bm, bn, n_out):
    m, k = x.shape
    n = n_out
    blk = bm * k * 2 + 2 * k * bn * 2 + bm * bn * 2
    return pl.pallas_call(
        functools.partial(_swiglu_kernel, n_valid=wg.shape[2]),
        out_shape=jax.ShapeDtypeStruct((m, n), BF16),
        grid=(m // bm, n // bn),
        in_specs=[
            pl.BlockSpec((bm, k), lambda i, j: (i, 0)),
            pl.BlockSpec((None, k, bn), lambda i, j: (layer, 0, j)),
            pl.BlockSpec((None, k, bn), lambda i, j: (layer, 0, j)),
        ],
        out_specs=pl.BlockSpec((bm, bn), lambda i, j: (i, j)),
        compiler_params=pltpu.CompilerParams(
            dimension_semantics=("arbitrary", "arbitrary"),
            vmem_limit_bytes=_vmem_limit(blk, 3 * bm * bn * 4)),
        name="ffn_swiglu",
    )(x, wg, wu)


def _mm_residual_kernel(x_ref, w_ref, r_ref, mod_ref, o_ref, acc_ref, *, gate_idx, nk):
    kk = pl.program_id(2)
    p = _dot(x_ref[...], w_ref[...])

    @pl.when(kk == 0)
    def _():
        acc_ref[...] = p

    @pl.when(kk > 0)
    def _():
        acc_ref[...] += p

    @pl.when(kk == nk - 1)
    def _():
        gate = mod_ref[gate_idx:gate_idx + 1, :]
        o_ref[...] = r_ref[...] + gate * acc_ref[...]


def _mm_residual(x, w, layer, res, mod, gate_idx, bm, bn, bk, ctx_rows, lat_len):
    m, k = x.shape
    n = w.shape[2]
    nk = k // bk
    blk = bm * bk * 2 + bk * bn * 2 + 2 * bm * bn * 4 + 6 * bn * 4
    return pl.pallas_call(
        functools.partial(_mm_residual_kernel, gate_idx=gate_idx, nk=nk),
        out_shape=jax.ShapeDtypeStruct((m, n), F32),
        grid=(m // bm, n // bn, nk),
        in_specs=[
            pl.BlockSpec((bm, bk), lambda i, j, kk: (i, kk)),
            pl.BlockSpec((None, bk, bn), lambda i, j, kk: (layer, kk, j)),
            pl.BlockSpec((bm, bn), lambda i, j, kk: (i, j)),
            pl.BlockSpec((None, 6, bn), lambda i, j, kk: (_mod_row(i * bm, ctx_rows, lat_len), 0, j)),
        ],
        out_specs=pl.BlockSpec((bm, bn), lambda i, j, kk: (i, j)),
        scratch_shapes=[pltpu.VMEM((bm, bn), F32)],
        compiler_params=pltpu.CompilerParams(
            dimension_semantics=("arbitrary", "arbitrary", "arbitrary"),
            vmem_limit_bytes=_vmem_limit(blk, 2 * bm * bn * 4)),
        name="residual_matmul",
    )(x, w, res, mod)


class _Segs:
    def __init__(self, ctx_seqs, ctx_len, lat_seqs, lat_len):
        self.ctx_sps = ctx_len // SEG
        self.lat_sps = lat_len // SEG
        self.ctx_seqs = ctx_seqs
        self.lat_seqs = lat_seqs
        self.ctx_segs = ctx_seqs * self.ctx_sps
        self.nseg = self.ctx_segs + lat_seqs * self.lat_sps

    def seg_of_step(self, i, reverse):
        return (self.nseg - 1 - i) if reverse else i

    def position(self, seg):
        is_ctx = seg < self.ctx_segs
        pos = jnp.where(is_ctx, seg % self.ctx_sps, (seg - self.ctx_segs) % self.lat_sps)
        sps = jnp.where(is_ctx, self.ctx_sps, self.lat_sps)
        return is_ctx, pos, sps

    def lat_seq(self, seg):
        return jnp.clip((seg - self.ctx_segs) // self.lat_sps, 0, self.lat_seqs - 1)

    def ctx_seq(self, seg):
        return jnp.clip(seg // self.ctx_sps, 0, self.ctx_seqs - 1)


def _seq_edges(segs, seg, reverse):
    is_ctx, pos, sps = segs.position(seg)
    first = pos == (sps - 1 if reverse else 0)
    last = pos == (0 if reverse else sps - 1)
    return is_ctx, first, last


def _hgrn_kernel(*refs, segs, heads, reverse):
    if reverse:
        q_ref, v_ref, f_ref, lb_ref, s0_ref, o_ref, sfin_ref, st_ref = refs
    else:
        (q_ref, v_ref, f_ref, lb_ref, s0_ref, gate_ref, obw_ref, gain_ref,
         o_ref, sfin_ref, st_ref) = refs
    seg = segs.seg_of_step(pl.program_id(0), reverse)
    is_ctx, first, last = _seq_edges(segs, seg, reverse)

    @pl.when(first & is_ctx)
    def _():
        st_ref[...] = jnp.zeros_like(st_ref)

    @pl.when(first & jnp.logical_not(is_ctx))
    def _():
        for h in range(heads):
            st_ref[h] = s0_ref[h].T

    _, causal = _chunk_masks(HGRN_CHUNK, reverse)
    tri = jnp.where(causal, 1.0, 0.0).astype(BF16)
    n_chunks = SEG // HGRN_CHUNK
    order = range(n_chunks - 1, -1, -1) if reverse else range(n_chunks)
    end_row = 0 if reverse else HGRN_CHUNK - 1
    sls = [slice(h * HEAD_DIM, (h + 1) * HEAD_DIM) for h in range(heads)]

    def run_heads(hs):
        ks, cums = {}, {}
        for h in hs:
            lb = lb_ref[:, sls[h]]
            f = lb + (1.0 - lb) * _sigmoid(f_ref[:, sls[h]])
            ks[h] = 1.0 - f
            cums[h] = _dot(tri, jnp.concatenate(_split2(jnp.log(f)), axis=1))

        yield
        qd, ki, kex, vb, decay = {}, {}, {}, {}, {}
        for h in hs:
            r = cums[h]
            b = r[:, :HEAD_DIM] + r[:, HEAD_DIM:]
            ends =[b[c * HGRN_CHUNK + end_row:c * HGRN_CHUNK + end_row + 1] for c in range(n_chunks)]
            bend = jnp.concatenate([jnp.broadcast_to(e, (HGRN_CHUNK, HEAD_DIM)) for e in ends], axis=0)
            qd[h] = (q_ref[:, sls[h]] * jnp.exp(b)).astype(BF16)
            ki[h] = (ks[h] * jnp.exp(-b)).astype(BF16)
            kex[h] = _block_expand((ks[h] * jnp.exp(bend - b)).astype(BF16), HGRN_CHUNK)
            vb[h] = v_ref[:, sls[h]].astype(BF16)
            decay[h] = [jnp.exp(e) for e in ends]

        yield
        att = {h: _dot_nt(qd[h], ki[h]) for h in hs}

        yield
        o_intra, kvt = {}, {}
        for h in hs:
            o_intra[h] = _dot(jnp.where(causal, att[h], 0.0).astype(BF16), vb[h])
            kvt[h] = _dot_tn(vb[h], kex[h])

        yield
        st_in = {h: [None] * n_chunks for h in hs}
        for h in hs:
            st = st_ref[h]
            for c in order:
                st_in[h][c] = st.astype(BF16)
                st = st * decay[h][c] + kvt[h][:, c * HEAD_DIM:(c + 1) * HEAD_DIM]
            st_ref[h] = st

        yield
        o_state = {h: [_dot_nt(qd[h][c * HGRN_CHUNK:(c + 1) * HGRN_CHUNK], st_in[h][c])
                       for c in range(n_chunks)] for h in hs}

        yield
        for h in hs:
            o = o_intra[h] + jnp.concatenate(o_state[h], axis=0)
            if reverse:
                o_ref[:, sls[h]] = o
            else:
                tot = o + obw_ref[:, sls[h]]
                y = tot * lax.rsqrt(jnp.mean(tot * tot, axis=-1, keepdims=True) + EPS) * gain_ref[:, sls[h]]
                g = gate_ref[:, sls[h]]
                o_ref[:, sls[h]] = (y * (g * _sigmoid(g))).astype(o_ref.dtype)

    _run_skewed([run_heads(range(g0, min(g0 + HGRN_GROUP, heads))) for g0 in range(0, heads, HGRN_GROUP)],
                HGRN_SKEW)

    @pl.when(last & is_ctx)
    def _():
        for h in range(heads):
            sfin_ref[h] = st_ref[h].T


def _hgrn_pass(proj, lb, state, layer, segs, reverse, *, gain=None, obw=None, y_cols=None):
    n = proj.shape[0]
    w = proj.shape[1] // 5
    heads = w // HEAD_DIM
    d = 1 if reverse else 0

    def seg_idx(i):
        return segs.seg_of_step(i, reverse)

    col = lambda cb: pl.BlockSpec((SEG, w), lambda i: (seg_idx(i), cb))
    in_specs = [
        col(0), col(1), col(4 if reverse else 3),
        pl.BlockSpec((None, 1, w), lambda i: (d, 0, 0)),
        pl.BlockSpec((None, None, None, heads, HEAD_DIM, HEAD_DIM),
                     lambda i: (segs.lat_seq(seg_idx(i)), layer, d, 0, 0, 0)),
    ]
    args = [proj, proj, proj, lb, state]
    sfin_shape = jax.ShapeDtypeStruct((segs.ctx_seqs, heads, HEAD_DIM, HEAD_DIM), F32)
    sfin_spec = pl.BlockSpec((None, heads, HEAD_DIM, HEAD_DIM),
                             lambda i: (segs.ctx_seq(seg_idx(i)), 0, 0, 0))
    blk = 3 * SEG * w * 4 + w * 4 + 2 * heads * HEAD_DIM * HEAD_DIM * 4
    if reverse:
        out_shape = [jax.ShapeDtypeStruct((n, w), F32), sfin_shape]
        out_specs = [pl.BlockSpec((SEG, w), lambda i: (seg_idx(i), 0)), sfin_spec]
        blk += SEG * w * 4
    else:
        in_specs += [col(2), pl.BlockSpec((SEG, w), lambda i: (i, 0)),
                     pl.BlockSpec((1, w), lambda i: (0, 0))]
        args += [proj, obw, gain.reshape(1, w)]
        out_shape = [jax.ShapeDtypeStruct((n, y_cols), BF16), sfin_shape]
        out_specs = [pl.BlockSpec((SEG, w), lambda i: (i, 0)), sfin_spec]
        blk += 2 * SEG * w * 4 + SEG * w * 2 + w * 4
    return pl.pallas_call(
        functools.partial(_hgrn_kernel, segs=segs, heads=heads, reverse=reverse),
        out_shape=out_shape,
        grid=(segs.nseg,),
        in_specs=in_specs,
        out_specs=out_specs,
        scratch_shapes=[pltpu.VMEM((heads, HEAD_DIM, HEAD_DIM), F32)],
        compiler_params=pltpu.CompilerParams(
            dimension_semantics=("arbitrary",),
            vmem_limit_bytes=_vmem_limit(blk, 16 * 1024 * 1024)),
        name="hgrn_bwd" if reverse else "hgrn_fwd",
    )(*args)


def _running_max(x, chunk, reverse):
    rows = x.shape[0]
    pos = lax.broadcasted_iota(jnp.int32, x.shape, 0) % chunk
    step = 1
    while step < chunk:
        if reverse:
            shifted = pltpu.roll(x, rows - step, axis=0)
            ok = pos < chunk - step
        else:
            shifted = pltpu.roll(x, step, axis=0)
            ok = pos >= step
        x = jnp.maximum(x, jnp.where(ok, shifted, -jnp.inf))
        step *= 2
    return x


def _mlstm_kernel(*refs, segs, heads, reverse):
    if reverse:
        (q_ref, k_ref, v_ref, gt_ref, gb_ref, c0_ref, n0_ref, m0_ref,
         o_ref, cfin_ref, nfin_ref, mfin_ref, ct_ref, n_ref, m_ref) = refs
    else:
        (q_ref, k_ref, v_ref, gt_ref, gb_ref, c0_ref, n0_ref, m0_ref, og_ref, hbw_ref, gain_ref, _y_in,
         o_ref, cfin_ref, nfin_ref, mfin_ref, ct_ref, n_ref, m_ref) = refs
    seg = segs.seg_of_step(pl.program_id(0), reverse)
    is_ctx, first, last = _seq_edges(segs, seg, reverse)
    lane0 = heads if reverse else 0

    @pl.when(first & is_ctx)
    def _():
        ct_ref[...] = jnp.zeros_like(ct_ref)
        n_ref[...] = jnp.zeros_like(n_ref)
        m_ref[...] = jnp.zeros_like(m_ref)

    @pl.when(first & jnp.logical_not(is_ctx))
    def _():
        for h in range(heads):
            ct_ref[h] = c0_ref[h].T
        n_ref[...] = n0_ref[...]
        m_ref[...] = m0_ref[...]

    half = GATE_LANES // 2
    gates = gt_ref[...] + gb_ref[...]
    ig = gates[:, :half]
    fpre = gates[:, half:]
    lf = jnp.minimum(fpre, 0.0) - jnp.log1p(jnp.exp(-jnp.abs(fpre)))
    _, causal = _chunk_masks(MLSTM_CHUNK, reverse)
    tri = jnp.where(causal, 1.0, 0.0).astype(BF16)
    r = _dot(tri, jnp.concatenate(_split3(lf), axis=1))
    b = r[:, :half] + r[:, half:2 * half] + r[:, 2 * half:]
    u = ig - b
    mrun = _running_max(u, MLSTM_CHUNK, reverse)
    n_chunks = SEG // MLSTM_CHUNK
    order = list(range(n_chunks - 1, -1, -1) if reverse else range(n_chunks))
    m_cur = m_ref[...]
    mu_parts, mul_parts, mcur_parts, dec_rows = ([None] * n_chunks for _ in range(4))
    for c in order:
        r0 = c * MLSTM_CHUNK
        end = r0 if reverse else r0 + MLSTM_CHUNK - 1
        mu_c = jnp.maximum(m_cur, mrun[r0:r0 + MLSTM_CHUNK])
        mu_end = mu_c[end - r0:end - r0 + 1]
        mu_parts[c] = mu_c
        mul_parts[c] = jnp.broadcast_to(mu_end, mu_c.shape)
        mcur_parts[c] = jnp.broadcast_to(m_cur, mu_c.shape)
        dec_rows[c] = jnp.exp(m_cur - mu_end)
        m_cur = b[end:end + 1] + mu_end
    m_ref[...] = m_cur
    mu = jnp.concatenate(mu_parts, axis=0)
    aw_all = jnp.exp(jnp.concatenate(mcur_parts, axis=0) - mu)
    emt_all = jnp.exp(-(b + mu))
    w_all = jnp.exp(u - jnp.concatenate(mul_parts, axis=0))
    u_t = u.T
    w_t = w_all.T

    scale = HEAD_DIM ** -0.5
    sls = [slice(h * HEAD_DIM, (h + 1) * HEAD_DIM) for h in range(heads)]
    chunk_rows = [slice(c * MLSTM_CHUNK, (c + 1) * MLSTM_CHUNK) for c in range(n_chunks)]
    ones_blk = jnp.ones((SEG, HEAD_DIM), BF16)
    row_is_chunk = (lax.broadcasted_iota(jnp.int32, (8, SEG), 0)
                    == lax.broadcasted_iota(jnp.int32, (8, SEG), 1) // MLSTM_CHUNK)

    def run_heads(hs):
        qs, kb, v1, scores = {}, {}, {}, {}
        for h in hs:
            qs[h] = (q_ref[:, sls[h]] * scale).astype(BF16)
            kb[h] = k_ref[:, sls[h]].astype(BF16)
            v1[h] = jnp.concatenate([v_ref[:, sls[h]].astype(BF16), ones_blk], axis=1)
            scores[h] = _dot_nt(qs[h], kb[h])

        yield
        ext, kvt, nsum = {}, {}, {}
        for h in hs:
            j = lane0 + h
            dmat = jnp.where(causal, jnp.exp(u_t[j:j + 1, :] - mu[:, j:j + 1]), 0.0)
            ext[h] = _dot((scores[h] * dmat).astype(BF16), v1[h])
            w_row = w_t[j:j + 1, :]
            vw_t = (v1[h][:, :HEAD_DIM].T.astype(F32) * w_row).astype(BF16)
            kvt[h] = _dot(vw_t, _block_expand(kb[h], MLSTM_CHUNK))
            w_rows = jnp.where(row_is_chunk, jnp.broadcast_to(w_row, (8, SEG)), 0.0).astype(BF16)
            nsum[h] = _dot(w_rows, kb[h])

        yield
        rhs = {h: [None] * n_chunks for h in hs}
        for h in hs:
            j = lane0 + h
            ct = ct_ref[h]
            nv = n_ref[h:h + 1, :]
            for c in order:
                rhs[h][c] = jnp.concatenate(
                    [ct, jnp.broadcast_to(nv, (HEAD_DIM, HEAD_DIM))], axis=0).astype(BF16)
                dec = dec_rows[c][:, j:j + 1]
                ct = dec * ct + kvt[h][:, c * HEAD_DIM:(c + 1) * HEAD_DIM]
                nv = dec * nv + nsum[h][c:c + 1]
            ct_ref[h] = ct
            n_ref[h:h + 1, :] = nv

        yield
        qc = {h: [_dot_nt(qs[h][chunk_rows[c]], rhs[h][c]) for c in range(n_chunks)] for h in hs}

        yield
        for h in hs:
            j = lane0 + h
            qcat = jnp.concatenate(qc[h], axis=0)
            aw = jnp.broadcast_to(aw_all[:, j:j + 1], (SEG, HEAD_DIM))
            emt = jnp.broadcast_to(emt_all[:, j:j + 1], (SEG, HEAD_DIM))
            num = aw * qcat[:, :HEAD_DIM] + ext[h][:, :HEAD_DIM]
            den = aw * qcat[:, HEAD_DIM:] + ext[h][:, HEAD_DIM:]
            hv = num / jnp.maximum(jnp.abs(den), emt)
            if reverse:
                o_ref[:, sls[h]] = hv
            else:
                tot = hv + hbw_ref[:, sls[h]]
                y = tot * lax.rsqrt(jnp.mean(tot * tot, axis=-1, keepdims=True) + EPS) * gain_ref[:, sls[h]]
                o_ref[:, sls[h]] = (_sigmoid(og_ref[:, sls[h]]) * y).astype(o_ref.dtype)

    _run_skewed([run_heads(range(g0, min(g0 + MLSTM_GROUP, heads))) for g0 in range(0, heads, MLSTM_GROUP)],
                MLSTM_SKEW)

    @pl.when(last & is_ctx)
    def _():
        for h in range(heads):
            cfin_ref[h] = ct_ref[h].T
        nfin_ref[...] = n_ref[...]
        mfin_ref[...] = m_ref[...]


def _mlstm_pass(proj, gate_b, c0, n0, m0, layer, segs, reverse, *, gain=None, hbw=None, y=None):
    n = proj.shape[0]
    w = (proj.shape[1] - GATE_LANES) // 4
    heads = w // HEAD_DIM
    d = 1 if reverse else 0

    def seg_idx(i):
        return segs.seg_of_step(i, reverse)

    def lat(i):
        return segs.lat_seq(seg_idx(i))

    col = lambda cb: pl.BlockSpec((SEG, w), lambda i: (seg_idx(i), cb))
    in_specs = [
        col(0), col(1), col(2),
        pl.BlockSpec((SEG, GATE_LANES), lambda i: (seg_idx(i), 4 * w // GATE_LANES)),
        pl.BlockSpec((1, GATE_LANES), lambda i: (0, 0)),
        pl.BlockSpec((None, None, None, heads, HEAD_DIM, HEAD_DIM), lambda i: (lat(i), layer, d, 0, 0, 0)),
        pl.BlockSpec((None, None, None, heads, HEAD_DIM), lambda i: (lat(i), layer, d, 0, 0)),
        pl.BlockSpec((None, None, 1, HEAD_DIM), lambda i: (lat(i), layer, 0, 0)),
    ]
    args = [proj, proj, proj, proj, gate_b, c0, n0, m0]
    fin_shapes = [
        jax.ShapeDtypeStruct((segs.ctx_seqs, heads, HEAD_DIM, HEAD_DIM), F32),
        jax.ShapeDtypeStruct((segs.ctx_seqs, heads, HEAD_DIM), F32),
        jax.ShapeDtypeStruct((segs.ctx_seqs, 1, HEAD_DIM), F32),
    ]

    def ctx(i):
        return segs.ctx_seq(seg_idx(i))

    fin_specs = [
        pl.BlockSpec((None, heads, HEAD_DIM, HEAD_DIM), lambda i: (ctx(i), 0, 0, 0)),
        pl.BlockSpec((None, heads, HEAD_DIM), lambda i: (ctx(i), 0, 0)),
        pl.BlockSpec((None, 1, HEAD_DIM), lambda i: (ctx(i), 0, 0)),
    ]
    blk = 3 * SEG * w * 4 + SEG * GATE_LANES * 4 + 2 * heads * HEAD_DIM * HEAD_DIM * 4
    if reverse:
        out_shape = [jax.ShapeDtypeStruct((n, w), F32)] + fin_shapes
        out_specs = [pl.BlockSpec((SEG, w), lambda i: (seg_idx(i), 0))] + fin_specs
        aliases = {}
        blk += SEG * w * 4
    else:
        in_specs += [col(3), pl.BlockSpec((SEG, w), lambda i: (i, 0)),
                     pl.BlockSpec((1, w), lambda i: (0, 0)),
                     pl.BlockSpec(memory_space=pl.ANY)]
        args += [proj, hbw, gain.reshape(1, w), y]
        out_shape = [jax.ShapeDtypeStruct(y.shape, y.dtype)] + fin_shapes
        out_specs = [pl.BlockSpec((SEG, w), lambda i: (i, 1))] + fin_specs
        aliases = {11: 0}
        blk += 2 * SEG * w * 4 + SEG * w * 2
    return pl.pallas_call(
        functools.partial(_mlstm_kernel, segs=segs, heads=heads, reverse=reverse),
        out_shape=out_shape,
        grid=(segs.nseg,),
        in_specs=in_specs,
        out_specs=out_specs,
        scratch_shapes=[pltpu.VMEM((heads, HEAD_DIM, HEAD_DIM), F32),
                        pltpu.VMEM((heads, HEAD_DIM), F32),
                        pltpu.VMEM((1, HEAD_DIM), F32)],
        input_output_aliases=aliases,
        compiler_params=pltpu.CompilerParams(
            dimension_semantics=("arbitrary",),
            vmem_limit_bytes=_vmem_limit(blk, 16 * 1024 * 1024)),
        name="mlstm_bwd" if reverse else "mlstm_fwd",
    )(*args)


def _gelu(x):
    return 0.5 * x * (1.0 + jnp.tanh(0.7978845608028654 * (x + 0.044715 * (x * x * x))))


def _cmlp_kernel(u_ref, v_ref, g_ref, w_ref, b_ref, _y_in, o_ref, *, groups):
    gv = _gelu(v_ref[...])
    vv = (gv * lax.rsqrt(jnp.mean(gv * gv, axis=-1, keepdims=True) + EPS) * g_ref[...]).astype(BF16)
    for g in range(groups):
        sl = slice(g * HEAD_DIM, (g + 1) * HEAD_DIM)
        wg = w_ref[g].astype(BF16)
        bias = b_ref[:, g:g + 1]
        for c in range(SEG // CMLP_CHUNK):
            rs = slice(c * CMLP_CHUNK, (c + 1) * CMLP_CHUNK)
            mixed = _dot(wg, vv[rs, sl]) + bias
            o_ref[rs, sl] = (_gelu(u_ref[rs, sl]) * mixed).astype(o_ref.dtype)


def _cmlp(proj, gain, ws, bs_t, layer, y, col_block):
    n = proj.shape[0]
    w = proj.shape[1] // 2
    groups = w // HEAD_DIM
    blk = 2 * SEG * w * 4 + w * 4 + groups * CMLP_CHUNK * CMLP_CHUNK * 4 + CMLP_CHUNK * 128 * 4 + SEG * w * 2
    return pl.pallas_call(
        functools.partial(_cmlp_kernel, groups=groups),
        out_shape=jax.ShapeDtypeStruct(y.shape, y.dtype),
        grid=(n // SEG,),
        in_specs=[
            pl.BlockSpec((SEG, w), lambda i: (i, 0)),
            pl.BlockSpec((SEG, w), lambda i: (i, 1)),
            pl.BlockSpec((1, w), lambda i: (0, 0)),
            pl.BlockSpec((None, groups, CMLP_CHUNK, CMLP_CHUNK), lambda i: (layer, 0, 0, 0)),
            pl.BlockSpec((None, CMLP_CHUNK, groups), lambda i: (layer, 0, 0)),
            pl.BlockSpec(memory_space=pl.ANY),
        ],
        out_specs=pl.BlockSpec((SEG, w), lambda i: (i, col_block)),
        input_output_aliases={5: 0},
        compiler_params=pltpu.CompilerParams(
            dimension_semantics=("arbitrary",), vmem_limit_bytes=_vmem_limit(blk, 8 * 1024 * 1024)),
        name="cmlp",
    )(proj, proj, gain.reshape(1, w), ws, bs_t, y)


def kernel(x_prompt, x_sample, state_hgrn, state_mlstm_c, state_mlstm_n, state_mlstm_m, c, c_ctx, w_in, mlstm_gate_b, hgrn_lb_logits, hgrn_onorm_g, mlstm_onorm_g, cmlp_vnorm_g, cmlp_ws, cmlp_bs, w_o, norm1_g, norm2_g, w_ada, b_ada, w_ffn_gate, w_ffn_up, w_ffn_down, final_g):
    batch, seq, d = x_prompt.shape
    dec_batch, dec_seq, _ = x_sample.shape
    depth = w_in.shape[0]
    wh = hgrn_onorm_g.shape[1]
    wm = mlstm_onorm_g.shape[1]
    wc = cmlp_vnorm_g.shape[1]
    hh, hm = wh // HEAD_DIM, wm // HEAD_DIM
    d_ff = w_ffn_gate.shape[2]
    assert seq % SEG == 0 and dec_seq % SEG == 0 and wh == wm
    assert 2 * hm <= GATE_LANES // 2 and (wh + wm) % wc == 0 and wh + wm + wc == d
    ctx_rows = batch * seq
    n = ctx_rows + dec_batch * dec_seq
    segs = _Segs(batch, seq, dec_batch, dec_seq)

    o_m = 5 * wh
    o_g = o_m + 4 * wm
    o_c = o_g + 4 * hm
    w_h = w_in[:, :, :o_m].astype(BF16)
    wg_cols = w_in[:, :, o_g:o_c]
    zpad = jnp.zeros((depth, d, GATE_LANES // 2 - 2 * hm), w_in.dtype)
    w_m = jnp.concatenate(
        [w_in[:, :, o_m:o_g], wg_cols[:, :, 0:hm], wg_cols[:, :, 2 * hm:3 * hm], zpad,
         wg_cols[:, :, hm:2 * hm], wg_cols[:, :, 3 * hm:4 * hm], zpad], axis=2).astype(BF16)
    w_c = w_in[:, :, o_c:].astype(BF16)
    bpad = jnp.zeros((depth, GATE_LANES // 2 - 2 * hm), F32)
    gb = mlstm_gate_b.astype(F32)
    gate_b = jnp.concatenate(
        [gb[:, 0:hm], gb[:, 2 * hm:3 * hm], bpad, gb[:, hm:2 * hm], gb[:, 3 * hm:4 * hm], bpad], axis=1)
    w_o_b = w_o.astype(BF16)
    ff_pad = (-d_ff) % 1024
    w_gate_b = w_ffn_gate.astype(BF16)
    w_up_b = w_ffn_up.astype(BF16)
    w_down_b = jnp.pad(w_ffn_down.astype(BF16), ((0, 0), (0, ff_pad), (0, 0)))
    d_ffp = d_ff + ff_pad

    lb_all = jnp.cumsum(jax.nn.softmax(hgrn_lb_logits.astype(F32), axis=0), axis=0)
    m0_rows = jnp.concatenate(
        [state_mlstm_m[:, :, 0, :], state_mlstm_m[:, :, 1, :],
         jnp.zeros((dec_batch, depth, HEAD_DIM - 2 * hm), F32)], axis=-1)[:, :, None, :]
    cmlp_bs_t = jnp.swapaxes(cmlp_bs, 1, 2)

    ada_rows = 16
    c_all = jnp.concatenate(
        [c_ctx[None, :], c, jnp.zeros((ada_rows - 1 - dec_batch, d), F32)], axis=0)
    mod_all = _ada_mod(c_all, w_ada, b_ada)[:, :1 + dec_batch].reshape(depth, 1 + dec_batch, 6, d)

    bm = _pick(min(ctx_rows, dec_seq), (1024, 512, 256))
    x = jnp.concatenate([x_prompt.reshape(ctx_rows, d), x_sample.reshape(dec_batch * dec_seq, d)], axis=0)
    new_h, new_c, new_n, new_m = [], [], [], []
    for l in range(depth):
        mod = mod_all[l]
        h = _normmod(x, norm1_g[l], mod, 0, 1, ctx_rows, dec_seq)
        proj_h = _mm_plain(h, w_h, l, bm, _pick(5 * wh, (768, 640, 512, 384, 256, 128)), F32)
        proj_m = _mm_plain(h, w_m, l, bm, _pick(4 * wm + GATE_LANES, (1280, 1024, 896, 640, 512, 256)), F32)
        proj_c = _mm_plain(h, w_c, l, bm, _pick(2 * wc, (1024, 512, 256)), F32)

        obw, sh_b = _hgrn_pass(proj_h, lb_all[l][:, None, :], state_hgrn, l, segs, True)
        y, sh_f = _hgrn_pass(proj_h, lb_all[l][:, None, :], state_hgrn, l, segs, False,
                             gain=hgrn_onorm_g[l], obw=obw, y_cols=d)
        hbw, sc_b, sn_b, sm_b = _mlstm_pass(proj_m, gate_b[l][None, :], state_mlstm_c, state_mlstm_n,
                                            m0_rows, l, segs, True)
        y, sc_f, sn_f, sm_f = _mlstm_pass(proj_m, gate_b[l][None, :], state_mlstm_c, state_mlstm_n,
                                          m0_rows, l, segs, False, gain=mlstm_onorm_g[l], hbw=hbw, y=y)
        y = _cmlp(proj_c, cmlp_vnorm_g[l], cmlp_ws, cmlp_bs_t, l, y, (wh + wm) // wc)
        new_h.append(jnp.stack([sh_f, sh_b], axis=1))
        new_c.append(jnp.stack([sc_f, sc_b], axis=1))
        new_n.append(jnp.stack([sn_f, sn_b], axis=1))
        new_m.append(jnp.stack([sm_f[:, 0, 0:hm], sm_b[:, 0, hm:2 * hm]], axis=1))

        x = _mm_residual(y, w_o_b, l, x, mod, 2, bm, _pick(d, (1024, 512, 256)), d, ctx_rows, dec_seq)
        h2 = _normmod(x, norm2_g[l], mod, 3, 4, ctx_rows, dec_seq)
        hid = _mm_swiglu(h2, w_gate_b, w_up_b, l, bm, _pick(d_ffp, (512, 256)), d_ffp)
        x = _mm_residual(hid, w_down_b, l, x, mod, 5, bm, _pick(d, (1024, 512, 256)),
                         _pick(d_ffp, (2816, 2048, 1536, 1024, 768, 512)), ctx_rows, dec_seq)

    y_prompt = _final_norm(x, final_g, 0, ctx_rows).reshape(batch, seq, d)
    y_sample = _final_norm(x, final_g, ctx_rows, dec_batch * dec_seq).reshape(dec_batch, dec_seq, d)
    return (y_prompt, y_sample, jnp.stack(new_h, axis=1), jnp.stack(new_c, axis=1),
            jnp.stack(new_n, axis=1), jnp.stack(new_m, axis=1))
```

```python
import functools

import jax
import jax.numpy as jnp
from jax import lax
from jax.experimental import pallas as pl
from jax.experimental.pallas import tpu as pltpu

F32 = jnp.float32
BF16 = jnp.bfloat16

HEAD_DIM = 128
HGRN_CHUNK = 32
MLSTM_CHUNK = 64
CMLP_CHUNK = 128
EPS = 1e-6
SEG = 256
GATE_LANES = 256
HGRN_GROUP, HGRN_SKEW = 3, 2
MLSTM_GROUP, MLSTM_SKEW = 6, 8
V7X_VMEM_BYTES = 64 * 1024 * 1024
VMEM_CAP_BYTES = V7X_VMEM_BYTES - 6 * 1024 * 1024


def _vmem_limit(block_bytes, temp_bytes=0):
    need = 2 * block_bytes + 2 * temp_bytes + 4 * 1024 * 1024
    return int(min(max(need, 16 * 1024 * 1024), VMEM_CAP_BYTES))


def _pick(n, candidates):
    for c in candidates:
        if c <= n and n % c == 0:
            return c
    return n


def _dot(a, b):
    return jnp.dot(a, b, preferred_element_type=F32)


def _dot_nt(a, b):
    return lax.dot_general(a, b, (((1,), (1,)), ((), ())), preferred_element_type=F32)


def _dot_tn(a, b):
    return lax.dot_general(a, b, (((0,), (0,)), ((), ())), preferred_element_type=F32)


def _sigmoid(x):
    return 0.5 * jnp.tanh(0.5 * x) + 0.5


def _split3(x):
    x1 = x.astype(BF16)
    r1 = x - x1.astype(F32)
    x2 = r1.astype(BF16)
    x3 = (r1 - x2.astype(F32)).astype(BF16)
    return x1, x2, x3


def _mean_sq_lanes(x):
    hi, lo = _split2(x * x)
    w = jnp.full((2 * HEAD_DIM, HEAD_DIM), 1.0 / HEAD_DIM, BF16)
    return _dot(jnp.concatenate([hi, lo], axis=1), w)


def _split2(x):
    x1 = x.astype(BF16)
    return x1, (x - x1.astype(F32)).astype(BF16)


def _chunk_masks(chunk, reverse):
    r = lax.broadcasted_iota(jnp.int32, (SEG, SEG), 0)
    c = lax.broadcasted_iota(jnp.int32, (SEG, SEG), 1)
    same = (r // chunk) == (c // chunk)
    causal = same & ((c >= r) if reverse else (c <= r))
    return same, causal


def _run_skewed(stage_gens, skew):
    live = list(enumerate(stage_gens))
    tick = 0
    while live:
        for item in list(live):
            g, gen = item
            if tick >= g * skew:
                try:
                    next(gen)
                except StopIteration:
                    live.remove(item)
        tick += 1


def _block_expand(x, chunk):
    rows, w = x.shape
    blocks = []
    for c in range(rows // chunk):
        parts = []
        if c > 0:
            parts.append(jnp.zeros((c * chunk, w), x.dtype))
        parts.append(x[c * chunk:(c + 1) * chunk])
        if (c + 1) * chunk < rows:
            parts.append(jnp.zeros((rows - (c + 1) * chunk, w), x.dtype))
        blocks.append(jnp.concatenate(parts, axis=0))
    return jnp.concatenate(blocks, axis=1)


def _ada_kernel(c_ref, w_ref, b_ref, o_ref):
    c = c_ref[...]
    a = (c * _sigmoid(c)).astype(BF16)
    o_ref[...] = _dot(a, w_ref[...].astype(BF16)) + b_ref[...]


def _ada_mod(c_all, w_ada, b_ada):
    depth, d, n6 = w_ada.shape
    rows = c_all.shape[0]
    tn = _pick(n6, (512, 256, 128))
    blk = rows * d * 4 + d * tn * 4 + tn * 4 + rows * tn * 4
    return pl.pallas_call(
        _ada_kernel,
        out_shape=jax.ShapeDtypeStruct((depth, rows, n6), F32),
        grid=(depth, n6 // tn),
        in_specs=[
            pl.BlockSpec((rows, d), lambda l, j: (0, 0)),
            pl.BlockSpec((None, d, tn), lambda l, j: (l, 0, j)),
            pl.BlockSpec((None, 1, tn), lambda l, j: (l, 0, j)),
        ],
        out_specs=pl.BlockSpec((None, rows, tn), lambda l, j: (l, 0, j)),
        compiler_params=pltpu.CompilerParams(
            dimension_semantics=("arbitrary", "arbitrary"),
            vmem_limit_bytes=_vmem_limit(blk, d * tn * 2)),
        name="ada_mod",
    )(c_all, w_ada, b_ada.reshape(depth, 1, n6))


def _mod_row(first_row, ctx_rows, lat_len):
    return jnp.where(first_row < ctx_rows, 0, 1 + (first_row - ctx_rows) // lat_len)


def _normmod_kernel(x_ref, g_ref, mod_ref, o_ref, *, shift_idx, scale_idx):
    x = x_ref[...]
    y = x * lax.rsqrt(jnp.mean(x * x, axis=-1, keepdims=True) + EPS) * g_ref[...]
    scale = mod_ref[scale_idx:scale_idx + 1, :]
    shift = mod_ref[shift_idx:shift_idx + 1, :]
    o_ref[...] = (y * (1.0 + scale) + shift).astype(o_ref.dtype)


def _normmod(x, gain, mod, shift_idx, scale_idx, ctx_rows, lat_len):
    n, d = x.shape
    rb = 2 * SEG if ctx_rows % (2 * SEG) == 0 and lat_len % (2 * SEG) == 0 else SEG
    blk = rb * d * 4 + d * 4 + 6 * d * 4 + rb * d * 2
    return pl.pallas_call(
        functools.partial(_normmod_kernel, shift_idx=shift_idx, scale_idx=scale_idx),
        out_shape=jax.ShapeDtypeStruct((n, d), BF16),
        grid=(n // rb,),
        in_specs=[
            pl.BlockSpec((rb, d), lambda i: (i, 0)),
            pl.BlockSpec((1, d), lambda i: (0, 0)),
            pl.BlockSpec((None, 6, d), lambda i: (_mod_row(i * rb, ctx_rows, lat_len), 0, 0)),
        ],
        out_specs=pl.BlockSpec((rb, d), lambda i: (i, 0)),
        compiler_params=pltpu.CompilerParams(
            dimension_semantics=("arbitrary",), vmem_limit_bytes=_vmem_limit(blk, 2 * rb * d * 4)),
        name="normmod",
    )(x, gain.reshape(1, d), mod)


def _final_norm_kernel(x_ref, g_ref, o_ref):
    x = x_ref[...]
    o_ref[...] = x * lax.rsqrt(jnp.mean(x * x, axis=-1, keepdims=True) + EPS) * g_ref[...]


def _final_norm(x, gain, row0, rows):
    d = x.shape[1]
    rb = 2 * SEG if row0 % (2 * SEG) == 0 and rows % (2 * SEG) == 0 else SEG
    blk = 2 * rb * d * 4 + d * 4
    return pl.pallas_call(
        _final_norm_kernel,
        out_shape=jax.ShapeDtypeStruct((rows, d), F32),
        grid=(rows // rb,),
        in_specs=[
            pl.BlockSpec((rb, d), lambda i: (row0 // rb + i, 0)),
            pl.BlockSpec((1, d), lambda i: (0, 0)),
        ],
        out_specs=pl.BlockSpec((rb, d), lambda i: (i, 0)),
        compiler_params=pltpu.CompilerParams(
            dimension_semantics=("arbitrary",), vmem_limit_bytes=_vmem_limit(blk, rb * d * 4)),
        name="final_norm",
    )(x, gain.reshape(1, d))


def _mm_kernel(x_ref, w_ref, o_ref):
    o_ref[...] = _dot(x_ref[...], w_ref[...]).astype(o_ref.dtype)


def _mm_plain(x, w, layer, bm, bn, out_dtype):
    m, k = x.shape
    n = w.shape[2]
    osz = jnp.dtype(out_dtype).itemsize
    blk = bm * k * 2 + k * bn * 2 + bm * bn * osz
    return pl.pallas_call(
        _mm_kernel,
        out_shape=jax.ShapeDtypeStruct((m, n), out_dtype),
        grid=(m // bm, n // bn),
        in_specs=[
            pl.BlockSpec((bm, k), lambda i, j: (i, 0)),
            pl.BlockSpec((None, k, bn), lambda i, j: (layer, 0, j)),
        ],
        out_specs=pl.BlockSpec((bm, bn), lambda i, j: (i, j)),
        compiler_params=pltpu.CompilerParams(
            dimension_semantics=("arbitrary", "arbitrary"),
            vmem_limit_bytes=_vmem_limit(blk, bm * bn * 4)),
        name="proj_matmul",
    )(x, w)


def _swiglu_kernel(x_ref, wg_ref, wu_ref, o_ref, *, n_valid):
    x = x_ref[...]
    g = _dot(x, wg_ref[...])
    u = _dot(x, wu_ref[...])
    val = g * _sigmoid(g) * u
    bn = o_ref.shape[1]
    col = pl.program_id(1) * bn + lax.broadcasted_iota(jnp.int32, val.shape, 1)
    o_ref[...] = jnp.where(col < n_valid, val, 0.0).astype(o_ref.dtype)


def _mm_swiglu(x, wg, wu, layer, bm, bn, n_out):
    m, k = x.shape
    n = n_out
    blk = bm * k * 2 + 2 * k * bn * 2 + bm * bn * 2
    return pl.pallas_call(
        functools.partial(_swiglu_kernel, n_valid=wg.shape[2]),
        out_shape=jax.ShapeDtypeStruct((m, n), BF16),
        grid=(m // bm, n // bn),
        in_specs=[
            pl.BlockSpec((bm, k), lambda i, j: (i, 0)),
            pl.BlockSpec((None, k, bn), lambda i, j: (layer, 0, j)),
            pl.BlockSpec((None, k, bn), lambda i, j: (layer, 0, j)),
        ],
        out_specs=pl.BlockSpec((bm, bn), lambda i, j: (i, j)),
        compiler_params=pltpu.CompilerParams(
            dimension_semantics=("arbitrary", "arbitrary"),
            vmem_limit_bytes=_vmem_limit(blk, 3 * bm * bn * 4)),
        name="ffn_swiglu",
    )(x, wg, wu)


def _mm_residual_kernel(x_ref, w_ref, r_ref, mod_ref, o_ref, acc_ref, *, gate_idx, nk):
    kk = pl.program_id(2)
    p = _dot(x_ref[...], w_ref[...])

    @pl.when(kk == 0)
    def _():
        acc_ref[...] = p

    @pl.when(kk > 0)
    def _():
        acc_ref[...] += p

    @pl.when(kk == nk - 1)
    def _():
        gate = mod_ref[gate_idx:gate_idx + 1, :]
        o_ref[...] = r_ref[...] + gate * acc_ref[...]


def _mm_residual(x, w, layer, res, mod, gate_idx, bm, bn, bk, ctx_rows, lat_len):
    m, k = x.shape
    n = w.shape[2]
    nk = k // bk
    blk = bm * bk * 2 + bk * bn * 2 + 2 * bm * bn * 4 + 6 * bn * 4
    return pl.pallas_call(
        functools.partial(_mm_residual_kernel, gate_idx=gate_idx, nk=nk),
        out_shape=jax.ShapeDtypeStruct((m, n), F32),
        grid=(m // bm, n // bn, nk),
        in_specs=[
            pl.BlockSpec((bm, bk), lambda i, j, kk: (i, kk)),
            pl.BlockSpec((None, bk, bn), lambda i, j, kk: (layer, kk, j)),
            pl.BlockSpec((bm, bn), lambda i, j, kk: (i, j)),
            pl.BlockSpec((None, 6, bn), lambda i, j, kk: (_mod_row(i * bm, ctx_rows, lat_len), 0, j)),
        ],
        out_specs=pl.BlockSpec((bm, bn), lambda i, j, kk: (i, j)),
        scratch_shapes=[pltpu.VMEM((bm, bn), F32)],
        compiler_params=pltpu.CompilerParams(
            dimension_semantics=("arbitrary", "arbitrary", "arbitrary"),
            vmem_limit_bytes=_vmem_limit(blk, 2 * bm * bn * 4)),
        name="residual_matmul",
    )(x, w, res, mod)


class _Segs:
    def __init__(self, ctx_seqs, ctx_len, lat_seqs, lat_len):
        self.ctx_sps = ctx_len // SEG
        self.lat_sps = lat_len // SEG
        self.ctx_seqs = ctx_seqs
        self.lat_seqs = lat_seqs
        self.ctx_segs = ctx_seqs * self.ctx_sps
        self.nseg = self.ctx_segs + lat_seqs * self.lat_sps

    def seg_of_step(self, i, reverse):
        return (self.nseg - 1 - i) if reverse else i

    def position(self, seg):
        is_ctx = seg < self.ctx_segs
        pos = jnp.where(is_ctx, seg % self.ctx_sps, (seg - self.ctx_segs) % self.lat_sps)
        sps = jnp.where(is_ctx, self.ctx_sps, self.lat_sps)
        return is_ctx, pos, sps

    def lat_seq(self, seg):
        return jnp.clip((seg - self.ctx_segs) // self.lat_sps, 0, self.lat_seqs - 1)

    def ctx_seq(self, seg):
        return jnp.clip(seg // self.ctx_sps, 0, self.ctx_seqs - 1)


def _seq_edges(segs, seg, reverse):
    is_ctx, pos, sps = segs.position(seg)
    first = pos == (sps - 1 if reverse else 0)
    last = pos == (0 if reverse else sps - 1)
    return is_ctx, first, last


def _hgrn_kernel(*refs, segs, heads, reverse):
    if reverse:
        q_ref, v_ref, f_ref, lb_ref, s0_ref, o_ref, sfin_ref, st_ref = refs
    else:
        (q_ref, v_ref, f_ref, lb_ref, s0_ref, gate_ref, obw_ref, gain_ref,
         o_ref, sfin_ref, st_ref) = refs
    seg = segs.seg_of_step(pl.program_id(0), reverse)
    is_ctx, first, last = _seq_edges(segs, seg, reverse)

    @pl.when(first & is_ctx)
    def _():
        st_ref[...] = jnp.zeros_like(st_ref)

    @pl.when(first & jnp.logical_not(is_ctx))
    def _():
        for h in range(heads):
            st_ref[h] = s0_ref[h].T

    if not reverse:
        own = heads * HEAD_DIM
        o_ref[:, own:] = jnp.zeros((SEG, o_ref.shape[1] - own), o_ref.dtype)

    _, causal = _chunk_masks(HGRN_CHUNK, reverse)
    tri = jnp.where(causal, 1.0, 0.0).astype(BF16)
    n_chunks = SEG // HGRN_CHUNK
    order = range(n_chunks - 1, -1, -1) if reverse else range(n_chunks)
    end_row = 0 if reverse else HGRN_CHUNK - 1
    sls = [slice(h * HEAD_DIM, (h + 1) * HEAD_DIM) for h in range(heads)]

    def run_heads(hs):
        ks, cums = {}, {}
        for h in hs:
            lb = lb_ref[:, sls[h]]
            f = lb + (1.0 - lb) * _sigmoid(f_ref[:, sls[h]])
            ks[h] = 1.0 - f
            cums[h] = _dot(tri, jnp.concatenate(_split2(jnp.log(f)), axis=1))

        yield
        qd, ki, kex, vb, decay = {}, {}, {}, {}, {}
        for h in hs:
            r = cums[h]
            b = r[:, :HEAD_DIM] + r[:, HEAD_DIM:]
            ends =[b[c * HGRN_CHUNK + end_row:c * HGRN_CHUNK + end_row + 1] for c in range(n_chunks)]
            bend = jnp.concatenate([jnp.broadcast_to(e, (HGRN_CHUNK, HEAD_DIM)) for e in ends], axis=0)
            qd[h] = (q_ref[:, sls[h]] * jnp.exp(b)).astype(BF16)
            ki[h] = (ks[h] * jnp.exp(-b)).astype(BF16)
            kex[h] = _block_expand((ks[h] * jnp.exp(bend - b)).astype(BF16), HGRN_CHUNK)
            vb[h] = v_ref[:, sls[h]].astype(BF16)
            decay[h] = [jnp.exp(e) for e in ends]

        yield
        att = {h: _dot_nt(qd[h], ki[h]) for h in hs}

        yield
        o_intra, kvt = {}, {}
        for h in hs:
            o_intra[h] = _dot(jnp.where(causal, att[h], 0.0).astype(BF16), vb[h])
            kvt[h] = _dot_tn(vb[h], kex[h])

        yield
        st_in = {h: [None] * n_chunks for h in hs}
        for h in hs:
            st = st_ref[h]
            for c in order:
                st_in[h][c] = st.astype(BF16)
                st = st * decay[h][c] + kvt[h][:, c * HEAD_DIM:(c + 1) * HEAD_DIM]
            st_ref[h] = st

        yield
        o_state = {h: [_dot_nt(qd[h][c * HGRN_CHUNK:(c + 1) * HGRN_CHUNK], st_in[h][c])
                       for c in range(n_chunks)] for h in hs}

        yield
        for h in hs:
            o = o_intra[h] + jnp.concatenate(o_state[h], axis=0)
            if reverse:
                o_ref[:, sls[h]] = o
            else:
                tot = o + obw_ref[:, sls[h]]
                y = tot * lax.rsqrt(jnp.mean(tot * tot, axis=-1, keepdims=True) + EPS) * gain_ref[:, sls[h]]
                g = gate_ref[:, sls[h]]
                o_ref[:, sls[h]] = (y * (g * _sigmoid(g))).astype(o_ref.dtype)

    _run_skewed([run_heads(range(g0, min(g0 + HGRN_GROUP, heads))) for g0 in range(0, heads, HGRN_GROUP)],
                HGRN_SKEW)

    @pl.when(last & is_ctx)
    def _():
        for h in range(heads):
            sfin_ref[h] = st_ref[h].T


def _hgrn_pass(proj, lb, state, layer, segs, reverse, *, gain=None, obw=None, y_cols=None):
    n = proj.shape[0]
    w = proj.shape[1] // 5
    heads = w // HEAD_DIM
    d = 1 if reverse else 0

    def seg_idx(i):
        return segs.seg_of_step(i, reverse)

    col = lambda cb: pl.BlockSpec((SEG, w), lambda i: (seg_idx(i), cb))
    in_specs = [
        col(0), col(1), col(4 if reverse else 3),
        pl.BlockSpec((None, 1, w), lambda i: (d, 0, 0)),
        pl.BlockSpec((None, None, None, heads, HEAD_DIM, HEAD_DIM),
                     lambda i: (segs.lat_seq(seg_idx(i)), layer, d, 0, 0, 0)),
    ]
    args = [proj, proj, proj, lb, state]
    sfin_shape = jax.ShapeDtypeStruct((segs.ctx_seqs, heads, HEAD_DIM, HEAD_DIM), F32)
    sfin_spec = pl.BlockSpec((None, heads, HEAD_DIM, HEAD_DIM),
                             lambda i: (segs.ctx_seq(seg_idx(i)), 0, 0, 0))
    blk = 3 * SEG * w * 4 + w * 4 + 2 * heads * HEAD_DIM * HEAD_DIM * 4
    if reverse:
        out_shape = [jax.ShapeDtypeStruct((n, w), F32), sfin_shape]
        out_specs = [pl.BlockSpec((SEG, w), lambda i: (seg_idx(i), 0)), sfin_spec]
        blk += SEG * w * 4
    else:
        in_specs += [col(2), pl.BlockSpec((SEG, w), lambda i: (i, 0)),
                     pl.BlockSpec((1, w), lambda i: (0, 0))]
        args += [proj, obw, gain.reshape(1, w)]
        out_shape = [jax.ShapeDtypeStruct((n, y_cols), BF16), sfin_shape]
        out_specs = [pl.BlockSpec((SEG, y_cols), lambda i: (i, 0)), sfin_spec]
        blk += 2 * SEG * w * 4 + SEG * y_cols * 2 + w * 4
    return pl.pallas_call(
        functools.partial(_hgrn_kernel, segs=segs, heads=heads, reverse=reverse),
        out_shape=out_shape,
        grid=(segs.nseg,),
        in_specs=in_specs,
        out_specs=out_specs,
        scratch_shapes=[pltpu.VMEM((heads, HEAD_DIM, HEAD_DIM), F32)],
        compiler_params=pltpu.CompilerParams(
            dimension_semantics=("arbitrary",),
            vmem_limit_bytes=_vmem_limit(blk, 16 * 1024 * 1024)),
        name="hgrn_bwd" if reverse else "hgrn_fwd",
    )(*args)


def _running_max(x, chunk, reverse):
    rows = x.shape[0]
    pos = lax.broadcasted_iota(jnp.int32, x.shape, 0) % chunk
    step = 1
    while step < chunk:
        if reverse:
            shifted = pltpu.roll(x, rows - step, axis=0)
            ok = pos < chunk - step
        else:
            shifted = pltpu.roll(x, step, axis=0)
            ok = pos >= step
        x = jnp.maximum(x, jnp.where(ok, shifted, -jnp.inf))
        step *= 2
    return x


def _mlstm_kernel(*refs, segs, heads, reverse):
    if reverse:
        (q_ref, k_ref, v_ref, gt_ref, gb_ref, c0_ref, n0_ref, m0_ref,
         o_ref, cfin_ref, nfin_ref, mfin_ref, ct_ref, n_ref, m_ref) = refs
    else:
        (q_ref, k_ref, v_ref, gt_ref, gb_ref, c0_ref, n0_ref, m0_ref, og_ref, hbw_ref, gain_ref, _y_in,
         o_ref, cfin_ref, nfin_ref, mfin_ref, ct_ref, n_ref, m_ref) = refs
    seg = segs.seg_of_step(pl.program_id(0), reverse)
    is_ctx, first, last = _seq_edges(segs, seg, reverse)
    lane0 = heads if reverse else 0

    @pl.when(first & is_ctx)
    def _():
        ct_ref[...] = jnp.zeros_like(ct_ref)
        n_ref[...] = jnp.zeros_like(n_ref)
        m_ref[...] = jnp.zeros_like(m_ref)

    @pl.when(first & jnp.logical_not(is_ctx))
    def _():
        for h in range(heads):
            ct_ref[h] = c0_ref[h].T
        n_ref[...] = n0_ref[...]
        m_ref[...] = m0_ref[...]

    half = GATE_LANES // 2
    gates = gt_ref[...] + gb_ref[...]
    ig = gates[:, :half]
    fpre = gates[:, half:]
    lf = jnp.minimum(fpre, 0.0) - jnp.log1p(jnp.exp(-jnp.abs(fpre)))
    _, causal = _chunk_masks(MLSTM_CHUNK, reverse)
    tri = jnp.where(causal, 1.0, 0.0).astype(BF16)
    r = _dot(tri, jnp.concatenate(_split3(lf), axis=1))
    b = r[:, :half] + r[:, half:2 * half] + r[:, 2 * half:]
    u = ig - b
    mrun = _running_max(u, MLSTM_CHUNK, reverse)
    n_chunks = SEG // MLSTM_CHUNK
    order = list(range(n_chunks - 1, -1, -1) if reverse else range(n_chunks))
    m_cur = m_ref[...]
    mu_parts, mul_parts, mcur_parts, dec_rows = ([None] * n_chunks for _ in range(4))
    for c in order:
        r0 = c * MLSTM_CHUNK
        end = r0 if reverse else r0 + MLSTM_CHUNK - 1
        mu_c = jnp.maximum(m_cur, mrun[r0:r0 + MLSTM_CHUNK])
        mu_end = mu_c[end - r0:end - r0 + 1]
        mu_parts[c] = mu_c
        mul_parts[c] = jnp.broadcast_to(mu_end, mu_c.shape)
        mcur_parts[c] = jnp.broadcast_to(m_cur, mu_c.shape)
        dec_rows[c] = jnp.exp(m_cur - mu_end)
        m_cur = b[end:end + 1] + mu_end
    m_ref[...] = m_cur
    mu = jnp.concatenate(mu_parts, axis=0)
    aw_all = jnp.exp(jnp.concatenate(mcur_parts, axis=0) - mu)
    emt_all = jnp.exp(-(b + mu))
    w_all = jnp.exp(u - jnp.concatenate(mul_parts, axis=0))
    u_t = u.T
    w_t = w_all.T

    scale = HEAD_DIM ** -0.5
    sls = [slice(h * HEAD_DIM, (h + 1) * HEAD_DIM) for h in range(heads)]
    chunk_rows = [slice(c * MLSTM_CHUNK, (c + 1) * MLSTM_CHUNK) for c in range(n_chunks)]
    ones_blk = jnp.ones((SEG, HEAD_DIM), BF16)
    row_is_chunk = (lax.broadcasted_iota(jnp.int32, (8, SEG), 0)
                    == lax.broadcasted_iota(jnp.int32, (8, SEG), 1) // MLSTM_CHUNK)

    def run_heads(hs):
        qs, kb, v1, scores = {}, {}, {}, {}
        for h in hs:
            qs[h] = (q_ref[:, sls[h]] * scale).astype(BF16)
            kb[h] = k_ref[:, sls[h]].astype(BF16)
            v1[h] = jnp.concatenate([v_ref[:, sls[h]].astype(BF16), ones_blk], axis=1)
            scores[h] = _dot_nt(qs[h], kb[h])

        yield
        ext, kvt, nsum = {}, {}, {}
        for h in hs:
            j = lane0 + h
            dmat = jnp.where(causal, jnp.exp(u_t[j:j + 1, :] - mu[:, j:j + 1]), 0.0)
            ext[h] = _dot((scores[h] * dmat).astype(BF16), v1[h])
            w_row = w_t[j:j + 1, :]
            vw_t = (v1[h][:, :HEAD_DIM].T.astype(F32) * w_row).astype(BF16)
            kvt[h] = _dot(vw_t, _block_expand(kb[h], MLSTM_CHUNK))
            w_rows = jnp.where(row_is_chunk, jnp.broadcast_to(w_row, (8, SEG)), 0.0).astype(BF16)
            nsum[h] = _dot(w_rows, kb[h])

        yield
        rhs = {h: [None] * n_chunks for h in hs}
        for h in hs:
            j = lane0 + h
            ct = ct_ref[h]
            nv = n_ref[h:h + 1, :]
            for c in order:
                rhs[h][c] = jnp.concatenate(
                    [ct, jnp.broadcast_to(nv, (HEAD_DIM, HEAD_DIM))], axis=0).astype(BF16)
                dec = dec_rows[c][:, j:j + 1]
                ct = dec * ct + kvt[h][:, c * HEAD_DIM:(c + 1) * HEAD_DIM]
                nv = dec * nv + nsum[h][c:c + 1]
            ct_ref[h] = ct
            n_ref[h:h + 1, :] = nv

        yield
        qc = {h: [_dot_nt(qs[h][chunk_rows[c]], rhs[h][c]) for c in range(n_chunks)] for h in hs}

        yield
        for h in hs:
            j = lane0 + h
            qcat = jnp.concatenate(qc[h], axis=0)
            aw = jnp.broadcast_to(aw_all[:, j:j + 1], (SEG, HEAD_DIM))
            emt = jnp.broadcast_to(emt_all[:, j:j + 1], (SEG, HEAD_DIM))
            num = aw * qcat[:, :HEAD_DIM] + ext[h][:, :HEAD_DIM]
            den = aw * qcat[:, HEAD_DIM:] + ext[h][:, HEAD_DIM:]
            hv = num / jnp.maximum(jnp.abs(den), emt)
            if reverse:
                o_ref[:, sls[h]] = hv
            else:
                tot = hv + hbw_ref[:, sls[h]]
                y = tot * lax.rsqrt(_mean_sq_lanes(tot) + EPS) * gain_ref[:, sls[h]]
                o_ref[:, sls[h]] = (_sigmoid(og_ref[:, sls[h]]) * y).astype(o_ref.dtype)

    _run_skewed([run_heads(range(g0, min(g0 + MLSTM_GROUP, heads))) for g0 in range(0, heads, MLSTM_GROUP)],
                MLSTM_SKEW)

    @pl.when(last & is_ctx)
    def _():
        for h in range(heads):
            cfin_ref[h] = ct_ref[h].T
        nfin_ref[...] = n_ref[...]
        mfin_ref[...] = m_ref[...]


def _mlstm_pass(proj, gate_b, c0, n0, m0, layer, segs, reverse, *, gain=None, hbw=None, y=None):
    n = proj.shape[0]
    w = (proj.shape[1] - GATE_LANES) // 4
    heads = w // HEAD_DIM
    d = 1 if reverse else 0

    def seg_idx(i):
        return segs.seg_of_step(i, reverse)

    def lat(i):
        return segs.lat_seq(seg_idx(i))

    col = lambda cb: pl.BlockSpec((SEG, w), lambda i: (seg_idx(i), cb))
    in_specs = [
        col(0), col(1), col(2),
        pl.BlockSpec((SEG, GATE_LANES), lambda i: (seg_idx(i), 4 * w // GATE_LANES)),
        pl.BlockSpec((1, GATE_LANES), lambda i: (0, 0)),
        pl.BlockSpec((None, None, None, heads, HEAD_DIM, HEAD_DIM), lambda i: (lat(i), layer, d, 0, 0, 0)),
        pl.BlockSpec((None, None, None, heads, HEAD_DIM), lambda i: (lat(i), layer, d, 0, 0)),
        pl.BlockSpec((None, None, 1, HEAD_DIM), lambda i: (lat(i), layer, 0, 0)),
    ]
    args = [proj, proj, proj, proj, gate_b, c0, n0, m0]
    fin_shapes = [
        jax.ShapeDtypeStruct((segs.ctx_seqs, heads, HEAD_DIM, HEAD_DIM), F32),
        jax.ShapeDtypeStruct((segs.ctx_seqs, heads, HEAD_DIM), F32),
        jax.ShapeDtypeStruct((segs.ctx_seqs, 1, HEAD_DIM), F32),
    ]

    def ctx(i):
        return segs.ctx_seq(seg_idx(i))

    fin_specs = [
        pl.BlockSpec((None, heads, HEAD_DIM, HEAD_DIM), lambda i: (ctx(i), 0, 0, 0)),
        pl.BlockSpec((None, heads, HEAD_DIM), lambda i: (ctx(i), 0, 0)),
        pl.BlockSpec((None, 1, HEAD_DIM), lambda i: (ctx(i), 0, 0)),
    ]
    blk = 3 * SEG * w * 4 + SEG * GATE_LANES * 4 + 2 * heads * HEAD_DIM * HEAD_DIM * 4
    if reverse:
        out_shape = [jax.ShapeDtypeStruct((n, w), F32)] + fin_shapes
        out_specs = [pl.BlockSpec((SEG, w), lambda i: (seg_idx(i), 0))] + fin_specs
        aliases = {}
        blk += SEG * w * 4
    else:
        in_specs += [col(3), pl.BlockSpec((SEG, w), lambda i: (i, 0)),
                     pl.BlockSpec((1, w), lambda i: (0, 0)),
                     pl.BlockSpec(memory_space=pl.ANY)]
        args += [proj, hbw, gain.reshape(1, w), y]
        out_shape = [jax.ShapeDtypeStruct(y.shape, y.dtype)] + fin_shapes
        out_specs = [pl.BlockSpec((SEG, w), lambda i: (i, 1))] + fin_specs
        aliases = {11: 0}
        blk += 2 * SEG * w * 4 + SEG * w * 2
    return pl.pallas_call(
        functools.partial(_mlstm_kernel, segs=segs, heads=heads, reverse=reverse),
        out_shape=out_shape,
        grid=(segs.nseg,),
        in_specs=in_specs,
        out_specs=out_specs,
        scratch_shapes=[pltpu.VMEM((heads, HEAD_DIM, HEAD_DIM), F32),
                        pltpu.VMEM((heads, HEAD_DIM), F32),
                        pltpu.VMEM((1, HEAD_DIM), F32)],
        input_output_aliases=aliases,
        compiler_params=pltpu.CompilerParams(
            dimension_semantics=("arbitrary",),
            vmem_limit_bytes=_vmem_limit(blk, 16 * 1024 * 1024)),
        name="mlstm_bwd" if reverse else "mlstm_fwd",
    )(*args)


def _gelu(x):
    return 0.5 * x * (1.0 + jnp.tanh(0.7978845608028654 * (x + 0.044715 * (x * x * x))))


def _cmlp_kernel(u_ref, v_ref, g_ref, w_ref, b_ref, _y_in, o_ref, *, groups):
    gv = _gelu(v_ref[...])
    vv = (gv * lax.rsqrt(jnp.mean(gv * gv, axis=-1, keepdims=True) + EPS) * g_ref[...]).astype(BF16)
    for g in range(groups):
        sl = slice(g * HEAD_DIM, (g + 1) * HEAD_DIM)
        wg = w_ref[g].astype(BF16)
        bias = b_ref[:, g:g + 1]
        for c in range(SEG // CMLP_CHUNK):
            rs = slice(c * CMLP_CHUNK, (c + 1) * CMLP_CHUNK)
            mixed = _dot(wg, vv[rs, sl]) + bias
            o_ref[rs, sl] = (_gelu(u_ref[rs, sl]) * mixed).astype(o_ref.dtype)


def _cmlp(proj, gain, ws, bs_t, layer, y, col_block):
    n = proj.shape[0]
    w = proj.shape[1] // 2
    groups = w // HEAD_DIM
    blk = 2 * SEG * w * 4 + w * 4 + groups * CMLP_CHUNK * CMLP_CHUNK * 4 + CMLP_CHUNK * 128 * 4 + SEG * w * 2
    return pl.pallas_call(
        functools.partial(_cmlp_kernel, groups=groups),
        out_shape=jax.ShapeDtypeStruct(y.shape, y.dtype),
        grid=(n // SEG,),
        in_specs=[
            pl.BlockSpec((SEG, w), lambda i: (i, 0)),
            pl.BlockSpec((SEG, w), lambda i: (i, 1)),
            pl.BlockSpec((1, w), lambda i: (0, 0)),
            pl.BlockSpec((None, groups, CMLP_CHUNK, CMLP_CHUNK), lambda i: (layer, 0, 0, 0)),
            pl.BlockSpec((None, CMLP_CHUNK, groups), lambda i: (layer, 0, 0)),
            pl.BlockSpec(memory_space=pl.ANY),
        ],
        out_specs=pl.BlockSpec((SEG, w), lambda i: (i, col_block)),
        input_output_aliases={5: 0},
        compiler_params=pltpu.CompilerParams(
            dimension_semantics=("arbitrary",), vmem_limit_bytes=_vmem_limit(blk, 8 * 1024 * 1024)),
        name="cmlp",
    )(proj, proj, gain.reshape(1, w), ws, bs_t, y)


def kernel(x_prompt, x_sample, state_hgrn, state_mlstm_c, state_mlstm_n, state_mlstm_m, c, c_ctx, w_in, mlstm_gate_b, hgrn_lb_logits, hgrn_onorm_g, mlstm_onorm_g, cmlp_vnorm_g, cmlp_ws, cmlp_bs, w_o, norm1_g, norm2_g, w_ada, b_ada, w_ffn_gate, w_ffn_up, w_ffn_down, final_g):
    batch, seq, d = x_prompt.shape
    dec_batch, dec_seq, _ = x_sample.shape
    depth = w_in.shape[0]
    wh = hgrn_onorm_g.shape[1]
    wm = mlstm_onorm_g.shape[1]
    wc = cmlp_vnorm_g.shape[1]
    hh, hm = wh // HEAD_DIM, wm // HEAD_DIM
    d_ff = w_ffn_gate.shape[2]
    assert seq % SEG == 0 and dec_seq % SEG == 0 and wh == wm
    assert 2 * hm <= GATE_LANES // 2 and (wh + wm) % wc == 0 and wh + wm + wc == d
    ctx_rows = batch * seq
    n = ctx_rows + dec_batch * dec_seq
    segs = _Segs(batch, seq, dec_batch, dec_seq)

    o_m = 5 * wh
    o_g = o_m + 4 * wm
    o_c = o_g + 4 * hm
    w_h = w_in[:, :, :o_m].astype(BF16)
    wg_cols = w_in[:, :, o_g:o_c]
    zpad = jnp.zeros((depth, d, GATE_LANES // 2 - 2 * hm), w_in.dtype)
    w_m = jnp.concatenate(
        [w_in[:, :, o_m:o_g], wg_cols[:, :, 0:hm], wg_cols[:, :, 2 * hm:3 * hm], zpad,
         wg_cols[:, :, hm:2 * hm], wg_cols[:, :, 3 * hm:4 * hm], zpad], axis=2).astype(BF16)
    w_c = w_in[:, :, o_c:].astype(BF16)
    bpad = jnp.zeros((depth, GATE_LANES // 2 - 2 * hm), F32)
    gb = mlstm_gate_b.astype(F32)
    gate_b = jnp.concatenate(
        [gb[:, 0:hm], gb[:, 2 * hm:3 * hm], bpad, gb[:, hm:2 * hm], gb[:, 3 * hm:4 * hm], bpad], axis=1)
    w_o_b = w_o.astype(BF16)
    ff_pad = (-d_ff) % 1024
    w_gate_b = w_ffn_gate.astype(BF16)
    w_up_b = w_ffn_up.astype(BF16)
    w_down_b = jnp.pad(w_ffn_down.astype(BF16), ((0, 0), (0, ff_pad), (0, 0)))
    d_ffp = d_ff + ff_pad

    lb_all = jnp.cumsum(jax.nn.softmax(hgrn_lb_logits.astype(F32), axis=0), axis=0)
    m0_rows = jnp.concatenate(
        [state_mlstm_m[:, :, 0, :], state_mlstm_m[:, :, 1, :],
         jnp.zeros((dec_batch, depth, HEAD_DIM - 2 * hm), F32)], axis=-1)[:, :, None, :]
    cmlp_bs_t = jnp.swapaxes(cmlp_bs, 1, 2)

    ada_rows = 16
    c_all = jnp.concatenate(
        [c_ctx[None, :], c, jnp.zeros((ada_rows - 1 - dec_batch, d), F32)], axis=0)
    mod_all = _ada_mod(c_all, w_ada, b_ada)[:, :1 + dec_batch].reshape(depth, 1 + dec_batch, 6, d)

    bm = _pick(min(ctx_rows, dec_seq), (1024, 512, 256))
    x = jnp.concatenate([x_prompt.reshape(ctx_rows, d), x_sample.reshape(dec_batch * dec_seq, d)], axis=0)
    new_h, new_c, new_n, new_m = [], [], [], []
    for l in range(depth):
        mod = mod_all[l]
        h = _normmod(x, norm1_g[l], mod, 0, 1, ctx_rows, dec_seq)
        proj_h = _mm_plain(h, w_h, l, bm, _pick(5 * wh, (768, 640, 512, 384, 256, 128)), F32)
        proj_m = _mm_plain(h, w_m, l, bm, _pick(4 * wm + GATE_LANES, (1280, 1024, 896, 640, 512, 256)), F32)
        proj_c = _mm_plain(h, w_c, l, bm, _pick(2 * wc, (1024, 512, 256)), F32)

        obw, sh_b = _hgrn_pass(proj_h, lb_all[l][:, None, :], state_hgrn, l, segs, True)
        y, sh_f = _hgrn_pass(proj_h, lb_all[l][:, None, :], state_hgrn, l, segs, False,
                             gain=hgrn_onorm_g[l], obw=obw, y_cols=d)
        hbw, sc_b, sn_b, sm_b = _mlstm_pass(proj_m, gate_b[l][None, :], state_mlstm_c, state_mlstm_n,
                                            m0_rows, l, segs, True)
        y, sc_f, sn_f, sm_f = _mlstm_pass(proj_m, gate_b[l][None, :], state_mlstm_c, state_mlstm_n,
                                          m0_rows, l, segs, False, gain=mlstm_onorm_g[l], hbw=hbw, y=y)
        y = _cmlp(proj_c, cmlp_vnorm_g[l], cmlp_ws, cmlp_bs_t, l, y, (wh + wm) // wc)
        new_h.append(jnp.stack([sh_f, sh_b], axis=1))
        new_c.append(jnp.stack([sc_f, sc_b], axis=1))
        new_n.append(jnp.stack([sn_f, sn_b], axis=1))
        new_m.append(jnp.stack([sm_f[:, 0, 0:hm], sm_b[:, 0, hm:2 * hm]], axis=1))

        x = _mm_residual(y, w_o_b, l, x, mod, 2, bm, _pick(d, (1024, 512, 256)), d, ctx_rows, dec_seq)
        h2 = _normmod(x, norm2_g[l], mod, 3, 4, ctx_rows, dec_seq)
        hid = _mm_swiglu(h2, w_gate_b, w_up_b, l, bm, _pick(d_ffp, (512, 256)), d_ffp)
        x = _mm_residual(hid, w_down_b, l, x, mod, 5, bm, _pick(d, (1024, 512, 256)),
                         _pick(d_ffp, (2816, 2048, 1536, 1024, 768, 512)), ctx_rows, dec_seq)

    y_prompt = _final_norm(x, final_g, 0, ctx_rows).reshape(batch, seq, d)
    y_sample = _final_norm(x, final_g, ctx_rows, dec_batch * dec_seq).reshape(dec_batch, dec_seq, d)
    return (y_prompt, y_sample, jnp.stack(new_h, axis=1), jnp.stack(new_c, axis=1),
            jnp.stack(new_n, axis=1), jnp.stack(new_m, axis=1))
```

```python
import functools

import jax
import jax.numpy as jnp
from jax import lax
from jax.experimental import pallas as pl
from jax.experimental.pallas import tpu as pltpu

F32 = jnp.float32
BF16 = jnp.bfloat16

HEAD_DIM = 128
HGRN_CHUNK = 32
MLSTM_CHUNK = 64
CMLP_CHUNK = 128
EPS = 1e-6
SEG = 256
GATE_LANES = 256
HGRN_GROUP, HGRN_SKEW = 3, 2
MLSTM_GROUP, MLSTM_SKEW = 6, 8
V7X_VMEM_BYTES = 64 * 1024 * 1024
VMEM_CAP_BYTES = V7X_VMEM_BYTES - 6 * 1024 * 1024


def _vmem_limit(block_bytes, temp_bytes=0):
    need = 2 * block_bytes + 2 * temp_bytes + 4 * 1024 * 1024
    return int(min(max(need, 16 * 1024 * 1024), VMEM_CAP_BYTES))


def _pick(n, candidates):
    for c in candidates:
        if c <= n and n % c == 0:
            return c
    return n


def _dot(a, b):
    return jnp.dot(a, b, preferred_element_type=F32)


def _dot_nt(a, b):
    return lax.dot_general(a, b, (((1,), (1,)), ((), ())), preferred_element_type=F32)


def _dot_tn(a, b):
    return lax.dot_general(a, b, (((0,), (0,)), ((), ())), preferred_element_type=F32)


def _sigmoid(x):
    return 0.5 * jnp.tanh(0.5 * x) + 0.5


def _split3(x):
    x1 = x.astype(BF16)
    r1 = x - x1.astype(F32)
    x2 = r1.astype(BF16)
    x3 = (r1 - x2.astype(F32)).astype(BF16)
    return x1, x2, x3


def _mean_sq_lanes(x):
    hi, lo = _split2(x * x)
    w = jnp.full((2 * HEAD_DIM, HEAD_DIM), 1.0 / HEAD_DIM, BF16)
    return _dot(jnp.concatenate([hi, lo], axis=1), w)


def _split2(x):
    x1 = x.astype(BF16)
    return x1, (x - x1.astype(F32)).astype(BF16)


def _chunk_masks(chunk, reverse):
    r = lax.broadcasted_iota(jnp.int32, (SEG, SEG), 0)
    c = lax.broadcasted_iota(jnp.int32, (SEG, SEG), 1)
    same = (r // chunk) == (c // chunk)
    causal = same & ((c >= r) if reverse else (c <= r))
    return same, causal


def _run_skewed(stage_gens, skew):
    live = list(enumerate(stage_gens))
    tick = 0
    while live:
        for item in list(live):
            g, gen = item
            if tick >= g * skew:
                try:
                    next(gen)
                except StopIteration:
                    live.remove(item)
        tick += 1


def _block_expand(x, chunk):
    rows, w = x.shape
    blocks = []
    for c in range(rows // chunk):
        parts = []
        if c > 0:
            parts.append(jnp.zeros((c * chunk, w), x.dtype))
        parts.append(x[c * chunk:(c + 1) * chunk])
        if (c + 1) * chunk < rows:
            parts.append(jnp.zeros((rows - (c + 1) * chunk, w), x.dtype))
        blocks.append(jnp.concatenate(parts, axis=0))
    return jnp.concatenate(blocks, axis=1)


def _ada_kernel(c_ref, w_ref, b_ref, o_ref):
    c = c_ref[...]
    a = (c * _sigmoid(c)).astype(BF16)
    o_ref[...] = _dot(a, w_ref[...].astype(BF16)) + b_ref[...]


def _ada_mod(c_all, w_ada, b_ada):
    depth, d, n6 = w_ada.shape
    rows = c_all.shape[0]
    tn = _pick(n6, (512, 256, 128))
    blk = rows * d * 4 + d * tn * 4 + tn * 4 + rows * tn * 4
    return pl.pallas_call(
        _ada_kernel,
        out_shape=jax.ShapeDtypeStruct((depth, rows, n6), F32),
        grid=(depth, n6 // tn),
        in_specs=[
            pl.BlockSpec((rows, d), lambda l, j: (0, 0)),
            pl.BlockSpec((None, d, tn), lambda l, j: (l, 0, j)),
            pl.BlockSpec((None, 1, tn), lambda l, j: (l, 0, j)),
        ],
        out_specs=pl.BlockSpec((None, rows, tn), lambda l, j: (l, 0, j)),
        compiler_params=pltpu.CompilerParams(
            dimension_semantics=("arbitrary", "arbitrary"),
            vmem_limit_bytes=_vmem_limit(blk, d * tn * 2)),
        name="ada_mod",
    )(c_all, w_ada, b_ada.reshape(depth, 1, n6))


def _mod_row(first_row, ctx_rows, lat_len):
    return jnp.where(first_row < ctx_rows, 0, 1 + (first_row - ctx_rows) // lat_len)


def _normmod_kernel(x_ref, g_ref, mod_ref, o_ref, *, shift_idx, scale_idx):
    x = x_ref[...]
    y = x * lax.rsqrt(jnp.mean(x * x, axis=-1, keepdims=True) + EPS) * g_ref[...]
    scale = mod_ref[scale_idx:scale_idx + 1, :]
    shift = mod_ref[shift_idx:shift_idx + 1, :]
    o_ref[...] = (y * (1.0 + scale) + shift).astype(o_ref.dtype)


def _normmod(x, gain, mod, shift_idx, scale_idx, ctx_rows, lat_len):
    n, d = x.shape
    rb = 2 * SEG if ctx_rows % (2 * SEG) == 0 and lat_len % (2 * SEG) == 0 else SEG
    blk = rb * d * 4 + d * 4 + 6 * d * 4 + rb * d * 2
    return pl.pallas_call(
        functools.partial(_normmod_kernel, shift_idx=shift_idx, scale_idx=scale_idx),
        out_shape=jax.ShapeDtypeStruct((n, d), BF16),
        grid=(n // rb,),
        in_specs=[
            pl.BlockSpec((rb, d), lambda i: (i, 0)),
            pl.BlockSpec((1, d), lambda i: (0, 0)),
            pl.BlockSpec((None, 6, d), lambda i: (_mod_row(i * rb, ctx_rows, lat_len), 0, 0)),
        ],
        out_specs=pl.BlockSpec((rb, d), lambda i: (i, 0)),
        compiler_params=pltpu.CompilerParams(
            dimension_semantics=("arbitrary",), vmem_limit_bytes=_vmem_limit(blk, 2 * rb * d * 4)),
        name="normmod",
    )(x, gain.reshape(1, d), mod)


def _final_norm_kernel(x_ref, g_ref, o_ref):
    x = x_ref[...]
    o_ref[...] = x * lax.rsqrt(jnp.mean(x * x, axis=-1, keepdims=True) + EPS) * g_ref[...]


def _final_norm(x, gain, row0, rows):
    d = x.shape[1]
    rb = 2 * SEG if row0 % (2 * SEG) == 0 and rows % (2 * SEG) == 0 else SEG
    blk = 2 * rb * d * 4 + d * 4
    return pl.pallas_call(
        _final_norm_kernel,
        out_shape=jax.ShapeDtypeStruct((rows, d), F32),
        grid=(rows // rb,),
        in_specs=[
            pl.BlockSpec((rb, d), lambda i: (row0 // rb + i, 0)),
            pl.BlockSpec((1, d), lambda i: (0, 0)),
        ],
        out_specs=pl.BlockSpec((rb, d), lambda i: (i, 0)),
        compiler_params=pltpu.CompilerParams(
            dimension_semantics=("arbitrary",), vmem_limit_bytes=_vmem_limit(blk, rb * d * 4)),
        name="final_norm",
    )(x, gain.reshape(1, d))


def _mm_kernel(x_ref, w_ref, o_ref):
    o_ref[...] = _dot(x_ref[...], w_ref[...]).astype(o_ref.dtype)


def _mm_plain(x, w, layer, bm, bn, out_dtype):
    m, k = x.shape
    n = w.shape[2]
    osz = jnp.dtype(out_dtype).itemsize
    blk = bm * k * 2 + k * bn * 2 + bm * bn * osz
    return pl.pallas_call(
        _mm_kernel,
        out_shape=jax.ShapeDtypeStruct((m, n), out_dtype),
        grid=(m // bm, n // bn),
        in_specs=[
            pl.BlockSpec((bm, k), lambda i, j: (i, 0)),
            pl.BlockSpec((None, k, bn), lambda i, j: (layer, 0, j)),
        ],
        out_specs=pl.BlockSpec((bm, bn), lambda i, j: (i, j)),
        compiler_params=pltpu.CompilerParams(
            dimension_semantics=("arbitrary", "arbitrary"),
            vmem_limit_bytes=_vmem_limit(blk, bm * bn * 4)),
        name="proj_matmul",
    )(x, w)


def _swiglu_kernel(x_ref, wg_ref, wu_ref, o_ref, *, n_valid):
    x = x_ref[...]
    g = _dot(x, wg_ref[...])
    u = _dot(x, wu_ref[...])
    val = g * _sigmoid(g) * u
    bn = o_ref.shape[1]
    col = pl.program_id(1) * bn + lax.broadcasted_iota(jnp.int32, val.shape, 1)
    o_ref[...] = jnp.where(col < n_valid, val, 0.0).astype(o_ref.dtype)


def _mm_swiglu(x, wg, wu, layer, bm, bn, n_out):
    m, k = x.shape
    n = n_out
    blk = bm * k * 2 + 2 * k * bn * 2 + bm * bn * 2
    return pl.pallas_call(
        functools.partial(_swiglu_kernel, n_valid=wg.shape[2]),
        out_shape=jax.ShapeDtypeStruct((m, n), BF16),
        grid=(m // bm, n // bn),
        in_specs=[
            pl.BlockSpec((bm, k), lambda i, j: (i, 0)),
            pl.BlockSpec((None, k, bn), lambda i, j: (layer, 0, j)),
            pl.BlockSpec((None, k, bn), lambda i, j: (layer, 0, j)),
        ],
        out_specs=pl.BlockSpec((bm, bn), lambda i, j: (i, j)),
        compiler_params=pltpu.CompilerParams(
            dimension_semantics=("arbitrary", "arbitrary"),
            vmem_limit_bytes=_vmem_limit(blk, 3 * bm * bn * 4)),
        name="ffn_swiglu",
    )(x, wg, wu)


def _mm_residual_kernel(x_ref, w_ref, r_ref, mod_ref, o_ref, acc_ref, *, gate_idx, nk):
    kk = pl.program_id(2)
    p = _dot(x_ref[...], w_ref[...])

    @pl.when(kk == 0)
    def _():
        acc_ref[...] = p

    @pl.when(kk > 0)
    def _():
        acc_ref[...] += p

    @pl.when(kk == nk - 1)
    def _():
        gate = mod_ref[gate_idx:gate_idx + 1, :]
        o_ref[...] = r_ref[...] + gate * acc_ref[...]


def _mm_residual(x, w, layer, res, mod, gate_idx, bm, bn, bk, ctx_rows, lat_len):
    m, k = x.shape
    n = w.shape[2]
    nk = k // bk
    blk = bm * bk * 2 + bk * bn * 2 + 2 * bm * bn * 4 + 6 * bn * 4
    return pl.pallas_call(
        functools.partial(_mm_residual_kernel, gate_idx=gate_idx, nk=nk),
        out_shape=jax.ShapeDtypeStruct((m, n), F32),
        grid=(m // bm, n // bn, nk),
        in_specs=[
            pl.BlockSpec((bm, bk), lambda i, j, kk: (i, kk)),
            pl.BlockSpec((None, bk, bn), lambda i, j, kk: (layer, kk, j)),
            pl.BlockSpec((bm, bn), lambda i, j, kk: (i, j)),
            pl.BlockSpec((None, 6, bn), lambda i, j, kk: (_mod_row(i * bm, ctx_rows, lat_len), 0, j)),
        ],
        out_specs=pl.BlockSpec((bm, bn), lambda i, j, kk: (i, j)),
        scratch_shapes=[pltpu.VMEM((bm, bn), F32)],
        compiler_params=pltpu.CompilerParams(
            dimension_semantics=("arbitrary", "arbitrary", "arbitrary"),
            vmem_limit_bytes=_vmem_limit(blk, 2 * bm * bn * 4)),
        name="residual_matmul",
    )(x, w, res, mod)


class _Segs:
    def __init__(self, ctx_seqs, ctx_len, lat_seqs, lat_len):
        self.ctx_sps = ctx_len // SEG
        self.lat_sps = lat_len // SEG
        self.ctx_seqs = ctx_seqs
        self.lat_seqs = lat_seqs
        self.ctx_segs = ctx_seqs * self.ctx_sps
        self.nseg = self.ctx_segs + lat_seqs * self.lat_sps

    def seg_of_step(self, i, reverse):
        return (self.nseg - 1 - i) if reverse else i

    def position(self, seg):
        is_ctx = seg < self.ctx_segs
        pos = jnp.where(is_ctx, seg % self.ctx_sps, (seg - self.ctx_segs) % self.lat_sps)
        sps = jnp.where(is_ctx, self.ctx_sps, self.lat_sps)
        return is_ctx, pos, sps

    def lat_seq(self, seg):
        return jnp.clip((seg - self.ctx_segs) // self.lat_sps, 0, self.lat_seqs - 1)

    def ctx_seq(self, seg):
        return jnp.clip(seg // self.ctx_sps, 0, self.ctx_seqs - 1)


def _seq_edges(segs, seg, reverse):
    is_ctx, pos, sps = segs.position(seg)
    first = pos == (sps - 1 if reverse else 0)
    last = pos == (0 if reverse else sps - 1)
    return is_ctx, first, last


def _hgrn_kernel(*refs, segs, heads, reverse):
    if reverse:
        q_ref, v_ref, f_ref, lb_ref, s0_ref, o_ref, sfin_ref, st_ref = refs
    else:
        (q_ref, v_ref, f_ref, lb_ref, s0_ref, gate_ref, obw_ref, gain_ref,
         o_ref, sfin_ref, st_ref) = refs
    seg = segs.seg_of_step(pl.program_id(0), reverse)
    is_ctx, first, last = _seq_edges(segs, seg, reverse)

    @pl.when(first & is_ctx)
    def _():
        st_ref[...] = jnp.zeros_like(st_ref)

    @pl.when(first & jnp.logical_not(is_ctx))
    def _():
        for h in range(heads):
            st_ref[h] = s0_ref[h].T

    if not reverse:
        own = heads * HEAD_DIM
        o_ref[:, own:] = jnp.zeros((SEG, o_ref.shape[1] - own), o_ref.dtype)

    _, causal = _chunk_masks(HGRN_CHUNK, reverse)
    tri = jnp.where(causal, 1.0, 0.0).astype(BF16)
    n_chunks = SEG // HGRN_CHUNK
    order = range(n_chunks - 1, -1, -1) if reverse else range(n_chunks)
    end_row = 0 if reverse else HGRN_CHUNK - 1
    sls = [slice(h * HEAD_DIM, (h + 1) * HEAD_DIM) for h in range(heads)]

    def run_heads(hs):
        ks, cums = {}, {}
        for h in hs:
            lb = lb_ref[:, sls[h]]
            f = lb + (1.0 - lb) * _sigmoid(f_ref[:, sls[h]])
            ks[h] = 1.0 - f
            cums[h] = _dot(tri, jnp.concatenate(_split2(jnp.log(f)), axis=1))

        yield
        qd, ki, kex, vb, decay = {}, {}, {}, {}, {}
        for h in hs:
            r = cums[h]
            b = r[:, :HEAD_DIM] + r[:, HEAD_DIM:]
            ends =[b[c * HGRN_CHUNK + end_row:c * HGRN_CHUNK + end_row + 1] for c in range(n_chunks)]
            bend = jnp.concatenate([jnp.broadcast_to(e, (HGRN_CHUNK, HEAD_DIM)) for e in ends], axis=0)
            qd[h] = (q_ref[:, sls[h]] * jnp.exp(b)).astype(BF16)
            ki[h] = (ks[h] * jnp.exp(-b)).astype(BF16)
            kex[h] = _block_expand((ks[h] * jnp.exp(bend - b)).astype(BF16), HGRN_CHUNK)
            vb[h] = v_ref[:, sls[h]].astype(BF16)
            decay[h] = [jnp.exp(e) for e in ends]

        yield
        att = {h: _dot_nt(qd[h], ki[h]) for h in hs}

        yield
        o_intra, kvt = {}, {}
        for h in hs:
            o_intra[h] = _dot(jnp.where(causal, att[h], 0.0).astype(BF16), vb[h])
            kvt[h] = _dot_tn(vb[h], kex[h])

        yield
        st_in = {h: [None] * n_chunks for h in hs}
        for h in hs:
            st = st_ref[h]
            for c in order:
                st_in[h][c] = st.astype(BF16)
                st = st * decay[h][c] + kvt[h][:, c * HEAD_DIM:(c + 1) * HEAD_DIM]
            st_ref[h] = st

        yield
        o_state = {h: [_dot_nt(qd[h][c * HGRN_CHUNK:(c + 1) * HGRN_CHUNK], st_in[h][c])
                       for c in range(n_chunks)] for h in hs}

        yield
        for h in hs:
            o = o_intra[h] + jnp.concatenate(o_state[h], axis=0)
            if reverse:
                o_ref[:, sls[h]] = o
            else:
                tot = o + obw_ref[:, sls[h]]
                y = tot * lax.rsqrt(jnp.mean(tot * tot, axis=-1, keepdims=True) + EPS) * gain_ref[:, sls[h]]
                g = gate_ref[:, sls[h]]
                o_ref[:, sls[h]] = (y * (g * _sigmoid(g))).astype(o_ref.dtype)

    _run_skewed([run_heads(range(g0, min(g0 + HGRN_GROUP, heads))) for g0 in range(0, heads, HGRN_GROUP)],
                HGRN_SKEW)

    @pl.when(last & is_ctx)
    def _():
        for h in range(heads):
            sfin_ref[h] = st_ref[h].T


def _hgrn_pass(proj, lb, state, layer, segs, reverse, *, gain=None, obw=None, y_cols=None):
    n = proj.shape[0]
    w = proj.shape[1] // 5
    heads = w // HEAD_DIM
    d = 1 if reverse else 0

    def seg_idx(i):
        return segs.seg_of_step(i, reverse)

    col = lambda cb: pl.BlockSpec((SEG, w), lambda i: (seg_idx(i), cb))
    in_specs = [
        col(0), col(1), col(4 if reverse else 3),
        pl.BlockSpec((None, 1, w), lambda i: (d, 0, 0)),
        pl.BlockSpec((None, None, None, heads, HEAD_DIM, HEAD_DIM),
                     lambda i: (segs.lat_seq(seg_idx(i)), layer, d, 0, 0, 0)),
    ]
    args = [proj, proj, proj, lb, state]
    sfin_shape = jax.ShapeDtypeStruct((segs.ctx_seqs, heads, HEAD_DIM, HEAD_DIM), F32)
    sfin_spec = pl.BlockSpec((None, heads, HEAD_DIM, HEAD_DIM),
                             lambda i: (segs.ctx_seq(seg_idx(i)), 0, 0, 0))
    blk = 3 * SEG * w * 4 + w * 4 + 2 * heads * HEAD_DIM * HEAD_DIM * 4
    if reverse:
        out_shape = [jax.ShapeDtypeStruct((n, w), F32), sfin_shape]
        out_specs = [pl.BlockSpec((SEG, w), lambda i: (seg_idx(i), 0)), sfin_spec]
        blk += SEG * w * 4
    else:
        in_specs += [col(2), pl.BlockSpec((SEG, w), lambda i: (i, 0)),
                     pl.BlockSpec((1, w), lambda i: (0, 0))]
        args += [proj, obw, gain.reshape(1, w)]
        out_shape = [jax.ShapeDtypeStruct((n, y_cols), BF16), sfin_shape]
        out_specs = [pl.BlockSpec((SEG, y_cols), lambda i: (i, 0)), sfin_spec]
        blk += 2 * SEG * w * 4 + SEG * y_cols * 2 + w * 4
    return pl.pallas_call(
        functools.partial(_hgrn_kernel, segs=segs, heads=heads, reverse=reverse),
        out_shape=out_shape,
        grid=(segs.nseg,),
        in_specs=in_specs,
        out_specs=out_specs,
        scratch_shapes=[pltpu.VMEM((heads, HEAD_DIM, HEAD_DIM), F32)],
        compiler_params=pltpu.CompilerParams(
            dimension_semantics=("arbitrary",),
            vmem_limit_bytes=_vmem_limit(blk, 16 * 1024 * 1024)),
        name="hgrn_bwd" if reverse else "hgrn_fwd",
    )(*args)


def _running_max(x, chunk, reverse):
    rows = x.shape[0]
    pos = lax.broadcasted_iota(jnp.int32, x.shape, 0) % chunk
    step = 1
    while step < chunk:
        if reverse:
            shifted = pltpu.roll(x, rows - step, axis=0)
            ok = pos < chunk - step
        else:
            shifted = pltpu.roll(x, step, axis=0)
            ok = pos >= step
        x = jnp.maximum(x, jnp.where(ok, shifted, -jnp.inf))
        step *= 2
    return x


def _mlstm_kernel(*refs, segs, heads, reverse):
    if reverse:
        (q_ref, k_ref, v_ref, gt_ref, gb_ref, c0_ref, n0_ref, m0_ref,
         o_ref, cfin_ref, nfin_ref, mfin_ref, ct_ref, n_ref, m_ref) = refs
    else:
        (q_ref, k_ref, v_ref, gt_ref, gb_ref, c0_ref, n0_ref, m0_ref, og_ref, hbw_ref, gain_ref, _y_in,
         o_ref, cfin_ref, nfin_ref, mfin_ref, ct_ref, n_ref, m_ref) = refs
    seg = segs.seg_of_step(pl.program_id(0), reverse)
    is_ctx, first, last = _seq_edges(segs, seg, reverse)
    lane0 = heads if reverse else 0

    @pl.when(first & is_ctx)
    def _():
        ct_ref[...] = jnp.zeros_like(ct_ref)
        n_ref[...] = jnp.zeros_like(n_ref)
        m_ref[...] = jnp.zeros_like(m_ref)

    @pl.when(first & jnp.logical_not(is_ctx))
    def _():
        for h in range(heads):
            ct_ref[h] = c0_ref[h].T
        n_ref[...] = n0_ref[...]
        m_ref[...] = m0_ref[...]

    half = GATE_LANES // 2
    gates = gt_ref[...] + gb_ref[...]
    ig = gates[:, :half]
    fpre = gates[:, half:]
    lf = jnp.minimum(fpre, 0.0) - jnp.log1p(jnp.exp(-jnp.abs(fpre)))
    _, causal = _chunk_masks(MLSTM_CHUNK, reverse)
    tri = jnp.where(causal, 1.0, 0.0).astype(BF16)
    r = _dot(tri, jnp.concatenate(_split3(lf), axis=1))
    b = r[:, :half] + r[:, half:2 * half] + r[:, 2 * half:]
    u = ig - b
    mrun = _running_max(u, MLSTM_CHUNK, reverse)
    n_chunks = SEG // MLSTM_CHUNK
    order = list(range(n_chunks - 1, -1, -1) if reverse else range(n_chunks))
    m_cur = m_ref[...]
    mu_parts, mul_parts, mcur_parts, dec_rows = ([None] * n_chunks for _ in range(4))
    for c in order:
        r0 = c * MLSTM_CHUNK
        end = r0 if reverse else r0 + MLSTM_CHUNK - 1
        mu_c = jnp.maximum(m_cur, mrun[r0:r0 + MLSTM_CHUNK])
        mu_end = mu_c[end - r0:end - r0 + 1]
        mu_parts[c] = mu_c
        mul_parts[c] = jnp.broadcast_to(mu_end, mu_c.shape)
        mcur_parts[c] = jnp.broadcast_to(m_cur, mu_c.shape)
        dec_rows[c] = jnp.exp(m_cur - mu_end)
        m_cur = b[end:end + 1] + mu_end
    m_ref[...] = m_cur
    mu = jnp.concatenate(mu_parts, axis=0)
    aw_all = jnp.exp(jnp.concatenate(mcur_parts, axis=0) - mu)
    emt_all = jnp.exp(-(b + mu))
    w_all = jnp.exp(u - jnp.concatenate(mul_parts, axis=0))
    u_t = u.T
    w_t = w_all.T

    scale = HEAD_DIM ** -0.5
    sls = [slice(h * HEAD_DIM, (h + 1) * HEAD_DIM) for h in range(heads)]
    chunk_rows = [slice(c * MLSTM_CHUNK, (c + 1) * MLSTM_CHUNK) for c in range(n_chunks)]
    ones_blk = jnp.ones((SEG, HEAD_DIM), BF16)
    row_is_chunk = (lax.broadcasted_iota(jnp.int32, (8, SEG), 0)
                    == lax.broadcasted_iota(jnp.int32, (8, SEG), 1) // MLSTM_CHUNK)

    def run_heads(hs):
        qs, kb, v1, scores = {}, {}, {}, {}
        for h in hs:
            qs[h] = (q_ref[:, sls[h]] * scale).astype(BF16)
            kb[h] = k_ref[:, sls[h]].astype(BF16)
            v1[h] = jnp.concatenate([v_ref[:, sls[h]].astype(BF16), ones_blk], axis=1)
            scores[h] = _dot_nt(qs[h], kb[h])

        yield
        ext, kvt, nsum = {}, {}, {}
        for h in hs:
            j = lane0 + h
            dmat = jnp.where(causal, jnp.exp(u_t[j:j + 1, :] - mu[:, j:j + 1]), 0.0)
            ext[h] = _dot((scores[h] * dmat).astype(BF16), v1[h])
            w_row = w_t[j:j + 1, :]
            vw_t = (v1[h][:, :HEAD_DIM].T.astype(F32) * w_row).astype(BF16)
            kvt[h] = _dot(vw_t, _block_expand(kb[h], MLSTM_CHUNK))
            w_rows = jnp.where(row_is_chunk, jnp.broadcast_to(w_row, (8, SEG)), 0.0).astype(BF16)
            nsum[h] = _dot(w_rows, kb[h])

        yield
        rhs = {h: [None] * n_chunks for h in hs}
        for h in hs:
            j = lane0 + h
            ct = ct_ref[h]
            nv = n_ref[h:h + 1, :]
            for c in order:
                rhs[h][c] = jnp.concatenate(
                    [ct, jnp.broadcast_to(nv, (HEAD_DIM, HEAD_DIM))], axis=0).astype(BF16)
                dec = dec_rows[c][:, j:j + 1]
                ct = dec * ct + kvt[h][:, c * HEAD_DIM:(c + 1) * HEAD_DIM]
                nv = dec * nv + nsum[h][c:c + 1]
            ct_ref[h] = ct
            n_ref[h:h + 1, :] = nv

        yield
        qc = {h: [_dot_nt(qs[h][chunk_rows[c]], rhs[h][c]) for c in range(n_chunks)] for h in hs}

        yield
        for h in hs:
            j = lane0 + h
            qcat = jnp.concatenate(qc[h], axis=0)
            aw = jnp.broadcast_to(aw_all[:, j:j + 1], (SEG, HEAD_DIM))
            emt = jnp.broadcast_to(emt_all[:, j:j + 1], (SEG, HEAD_DIM))
            num = aw * qcat[:, :HEAD_DIM] + ext[h][:, :HEAD_DIM]
            den = aw * qcat[:, HEAD_DIM:] + ext[h][:, HEAD_DIM:]
            hv = num / jnp.maximum(jnp.abs(den), emt)
            if reverse:
                o_ref[:, sls[h]] = hv
            else:
                tot = hv + hbw_ref[:, sls[h]]
                y = tot * lax.rsqrt(_mean_sq_lanes(tot) + EPS) * gain_ref[:, sls[h]]
                o_ref[:, sls[h]] = (_sigmoid(og_ref[:, sls[h]]) * y).astype(o_ref.dtype)

    _run_skewed([run_heads(range(g0, min(g0 + MLSTM_GROUP, heads))) for g0 in range(0, heads, MLSTM_GROUP)],
                MLSTM_SKEW)

    @pl.when(last & is_ctx)
    def _():
        for h in range(heads):
            cfin_ref[h] = ct_ref[h].T
        nfin_ref[...] = n_ref[...]
        mfin_ref[...] = m_ref[...]


def _mlstm_pass(proj, gate_b, c0, n0, m0, layer, segs, reverse, *, gain=None, hbw=None, y=None):
    n = proj.shape[0]
    w = (proj.shape[1] - GATE_LANES) // 4
    heads = w // HEAD_DIM
    d = 1 if reverse else 0

    def seg_idx(i):
        return segs.seg_of_step(i, reverse)

    def lat(i):
        return segs.lat_seq(seg_idx(i))

    col = lambda cb: pl.BlockSpec((SEG, w), lambda i: (seg_idx(i), cb))
    in_specs = [
        col(0), col(1), col(2),
        pl.BlockSpec((SEG, GATE_LANES), lambda i: (seg_idx(i), 4 * w // GATE_LANES)),
        pl.BlockSpec((1, GATE_LANES), lambda i: (0, 0)),
        pl.BlockSpec((None, None, None, heads, HEAD_DIM, HEAD_DIM), lambda i: (lat(i), layer, d, 0, 0, 0)),
        pl.BlockSpec((None, None, None, heads, HEAD_DIM), lambda i: (lat(i), layer, d, 0, 0)),
        pl.BlockSpec((None, None, 1, HEAD_DIM), lambda i: (lat(i), layer, 0, 0)),
    ]
    args = [proj, proj, proj, proj, gate_b, c0, n0, m0]
    fin_shapes = [
        jax.ShapeDtypeStruct((segs.ctx_seqs, heads, HEAD_DIM, HEAD_DIM), F32),
        jax.ShapeDtypeStruct((segs.ctx_seqs, heads, HEAD_DIM), F32),
        jax.ShapeDtypeStruct((segs.ctx_seqs, 1, HEAD_DIM), F32),
    ]

    def ctx(i):
        return segs.ctx_seq(seg_idx(i))

    fin_specs = [
        pl.BlockSpec((None, heads, HEAD_DIM, HEAD_DIM), lambda i: (ctx(i), 0, 0, 0)),
        pl.BlockSpec((None, heads, HEAD_DIM), lambda i: (ctx(i), 0, 0)),
        pl.BlockSpec((None, 1, HEAD_DIM), lambda i: (ctx(i), 0, 0)),
    ]
    blk = 3 * SEG * w * 4 + SEG * GATE_LANES * 4 + 2 * heads * HEAD_DIM * HEAD_DIM * 4
    if reverse:
        out_shape = [jax.ShapeDtypeStruct((n, w), F32)] + fin_shapes
        out_specs = [pl.BlockSpec((SEG, w), lambda i: (seg_idx(i), 0))] + fin_specs
        aliases = {}
        blk += SEG * w * 4
    else:
        in_specs += [col(3), pl.BlockSpec((SEG, w), lambda i: (i, 0)),
                     pl.BlockSpec((1, w), lambda i: (0, 0)),
                     pl.BlockSpec(memory_space=pl.ANY)]
        args += [proj, hbw, gain.reshape(1, w), y]
        out_shape = [jax.ShapeDtypeStruct(y.shape, y.dtype)] + fin_shapes
        out_specs = [pl.BlockSpec((SEG, w), lambda i: (i, 1))] + fin_specs
        aliases = {11: 0}
        blk += 2 * SEG * w * 4 + SEG * w * 2
    return pl.pallas_call(
        functools.partial(_mlstm_kernel, segs=segs, heads=heads, reverse=reverse),
        out_shape=out_shape,
        grid=(segs.nseg,),
        in_specs=in_specs,
        out_specs=out_specs,
        scratch_shapes=[pltpu.VMEM((heads, HEAD_DIM, HEAD_DIM), F32),
                        pltpu.VMEM((heads, HEAD_DIM), F32),
                        pltpu.VMEM((1, HEAD_DIM), F32)],
        input_output_aliases=aliases,
        compiler_params=pltpu.CompilerParams(
            dimension_semantics=("arbitrary",),
            vmem_limit_bytes=_vmem_limit(blk, 16 * 1024 * 1024)),
        name="mlstm_bwd" if reverse else "mlstm_fwd",
    )(*args)


def _gelu(x):
    return 0.5 * x * (1.0 + jnp.tanh(0.7978845608028654 * (x + 0.044715 * (x * x * x))))


def _cmlp_kernel(u_ref, v_ref, g_ref, w_ref, b_ref, _y_in, o_ref, *, groups):
    gv = _gelu(v_ref[...])
    vv = (gv * lax.rsqrt(jnp.mean(gv * gv, axis=-1, keepdims=True) + EPS) * g_ref[...]).astype(BF16)
    for g in range(groups):
        sl = slice(g * HEAD_DIM, (g + 1) * HEAD_DIM)
        wg = w_ref[g].astype(BF16)
        bias = b_ref[:, g:g + 1]
        for c in range(SEG // CMLP_CHUNK):
            rs = slice(c * CMLP_CHUNK, (c + 1) * CMLP_CHUNK)
            mixed = _dot(wg, vv[rs, sl]) + bias
            o_ref[rs, sl] = (_gelu(u_ref[rs, sl]) * mixed).astype(o_ref.dtype)


def _cmlp(proj, gain, ws, bs_t, layer, y, col_block):
    n = proj.shape[0]
    w = proj.shape[1] // 2
    groups = w // HEAD_DIM
    blk = 2 * SEG * w * 4 + w * 4 + groups * CMLP_CHUNK * CMLP_CHUNK * 4 + CMLP_CHUNK * 128 * 4 + SEG * w * 2
    return pl.pallas_call(
        functools.partial(_cmlp_kernel, groups=groups),
        out_shape=jax.ShapeDtypeStruct(y.shape, y.dtype),
        grid=(n // SEG,),
        in_specs=[
            pl.BlockSpec((SEG, w), lambda i: (i, 0)),
            pl.BlockSpec((SEG, w), lambda i: (i, 1)),
            pl.BlockSpec((1, w), lambda i: (0, 0)),
            pl.BlockSpec((None, groups, CMLP_CHUNK, CMLP_CHUNK), lambda i: (layer, 0, 0, 0)),
            pl.BlockSpec((None, CMLP_CHUNK, groups), lambda i: (layer, 0, 0)),
            pl.BlockSpec(memory_space=pl.ANY),
        ],
        out_specs=pl.BlockSpec((SEG, w), lambda i: (i, col_block)),
        input_output_aliases={5: 0},
        compiler_params=pltpu.CompilerParams(
            dimension_semantics=("arbitrary",), vmem_limit_bytes=_vmem_limit(blk, 8 * 1024 * 1024)),
        name="cmlp",
    )(proj, proj, gain.reshape(1, w), ws, bs_t, y)


def kernel(x_prompt, x_sample, state_hgrn, state_mlstm_c, state_mlstm_n, state_mlstm_m, c, c_ctx, w_in, mlstm_gate_b, hgrn_lb_logits, hgrn_onorm_g, mlstm_onorm_g, cmlp_vnorm_g, cmlp_ws, cmlp_bs, w_o, norm1_g, norm2_g, w_ada, b_ada, w_ffn_gate, w_ffn_up, w_ffn_down, final_g):
    batch, seq, d = x_prompt.shape
    dec_batch, dec_seq, _ = x_sample.shape
    depth = w_in.shape[0]
    wh = hgrn_onorm_g.shape[1]
    wm = mlstm_onorm_g.shape[1]
    wc = cmlp_vnorm_g.shape[1]
    hh, hm = wh // HEAD_DIM, wm // HEAD_DIM
    d_ff = w_ffn_gate.shape[2]
    assert seq % SEG == 0 and dec_seq % SEG == 0 and wh == wm
    assert 2 * hm <= GATE_LANES // 2 and (wh + wm) % wc == 0 and wh + wm + wc == d
    ctx_rows = batch * seq
    n = ctx_rows + dec_batch * dec_seq
    segs = _Segs(batch, seq, dec_batch, dec_seq)

    o_m = 5 * wh
    o_g = o_m + 4 * wm
    o_c = o_g + 4 * hm
    w_h = w_in[:, :, :o_m].astype(BF16)
    wg_cols = w_in[:, :, o_g:o_c]
    zpad = jnp.zeros((depth, d, GATE_LANES // 2 - 2 * hm), w_in.dtype)
    w_m = jnp.concatenate(
        [w_in[:, :, o_m:o_g], wg_cols[:, :, 0:hm], wg_cols[:, :, 2 * hm:3 * hm], zpad,
         wg_cols[:, :, hm:2 * hm], wg_cols[:, :, 3 * hm:4 * hm], zpad], axis=2).astype(BF16)
    w_c = w_in[:, :, o_c:].astype(BF16)
    bpad = jnp.zeros((depth, GATE_LANES // 2 - 2 * hm), F32)
    gb = mlstm_gate_b.astype(F32)
    gate_b = jnp.concatenate(
        [gb[:, 0:hm], gb[:, 2 * hm:3 * hm], bpad, gb[:, hm:2 * hm], gb[:, 3 * hm:4 * hm], bpad], axis=1)
    w_o_b = w_o.astype(BF16)
    ff_pad = (-d_ff) % 1024
    w_gate_b = w_ffn_gate.astype(BF16)
    w_up_b = w_ffn_up.astype(BF16)
    w_down_b = jnp.pad(w_ffn_down.astype(BF16), ((0, 0), (0, ff_pad), (0, 0)))
    d_ffp = d_ff + ff_pad

    lb_all = jnp.cumsum(jax.nn.softmax(hgrn_lb_logits.astype(F32), axis=0), axis=0)
    m0_rows = jnp.concatenate(
        [state_mlstm_m[:, :, 0, :], state_mlstm_m[:, :, 1, :],
         jnp.zeros((dec_batch, depth, HEAD_DIM - 2 * hm), F32)], axis=-1)[:, :, None, :]
    cmlp_bs_t = jnp.swapaxes(cmlp_bs, 1, 2)

    ada_rows = 16
    c_all = jnp.concatenate(
        [c_ctx[None, :], c, jnp.zeros((ada_rows - 1 - dec_batch, d), F32)], axis=0)
    mod_all = _ada_mod(c_all, w_ada, b_ada)[:, :1 + dec_batch].reshape(depth, 1 + dec_batch, 6, d)

    bm = _pick(min(ctx_rows, dec_seq), (1024, 512, 256))
    x = jnp.concatenate([x_prompt.reshape(ctx_rows, d), x_sample.reshape(dec_batch * dec_seq, d)], axis=0)
    new_h, new_c, new_n, new_m = [], [], [], []
    for l in range(depth):
        mod = mod_all[l]
        h = _normmod(x, norm1_g[l], mod, 0, 1, ctx_rows, dec_seq)
        proj_h = _mm_plain(h, w_h, l, bm, _pick(5 * wh, (768, 640, 512, 384, 256, 128)), F32)
        proj_m = _mm_plain(h, w_m, l, bm, _pick(4 * wm + GATE_LANES, (1280, 1024, 896, 640, 512, 256)), F32)
        proj_c = _mm_plain(h, w_c, l, bm, _pick(2 * wc, (1024, 512, 256)), F32)

        obw, sh_b = _hgrn_pass(proj_h, lb_all[l][:, None, :], state_hgrn, l, segs, True)
        y, sh_f = _hgrn_pass(proj_h, lb_all[l][:, None, :], state_hgrn, l, segs, False,
                             gain=hgrn_onorm_g[l], obw=obw, y_cols=d)
        hbw, sc_b, sn_b, sm_b = _mlstm_pass(proj_m, gate_b[l][None, :], state_mlstm_c, state_mlstm_n,
                                            m0_rows, l, segs, True)
        y, sc_f, sn_f, sm_f = _mlstm_pass(proj_m, gate_b[l][None, :], state_mlstm_c, state_mlstm_n,
                                          m0_rows, l, segs, False, gain=mlstm_onorm_g[l], hbw=hbw, y=y)
        y = _cmlp(proj_c, cmlp_vnorm_g[l], cmlp_ws, cmlp_bs_t, l, y, (wh + wm) // wc)
        new_h.append(jnp.stack([sh_f, sh_b], axis=1))
        new_c.append(jnp.stack([sc_f, sc_b], axis=1))
        new_n.append(jnp.stack([sn_f, sn_b], axis=1))
        new_m.append(jnp.stack([sm_f[:, 0, 0:hm], sm_b[:, 0, hm:2 * hm]], axis=1))

        x = _mm_residual(y, w_o_b, l, x, mod, 2, bm, _pick(d, (1024, 512, 256)), d, ctx_rows, dec_seq)
        h2 = _normmod(x, norm2_g[l], mod, 3, 4, ctx_rows, dec_seq)
        hid = _mm_swiglu(h2, w_gate_b, w_up_b, l, bm, _pick(d_ffp, (512, 256)), d_ffp)
        x = _mm_residual(hid, w_down_b, l, x, mod, 5, min(bm, 512), _pick(d, (512, 256)),
                         d_ffp, ctx_rows, dec_seq)

    y_prompt = _final_norm(x, final_g, 0, ctx_rows).reshape(batch, seq, d)
    y_sample = _final_norm(x, final_g, ctx_rows, dec_batch * dec_seq).reshape(dec_batch, dec_seq, d)
    return (y_prompt, y_sample, jnp.stack(new_h, axis=1), jnp.stack(new_c, axis=1),
            jnp.stack(new_n, axis=1), jnp.stack(new_m, axis=1))
```
